```python
import math, functools
import jax, jax.numpy as jnp
from jax import lax
import numpy as np

D_MODEL = 1024
BATCH = 4
SEQ = 4096
DEPTH = 2
DEC_BATCH = 128
DEC_SEQ = 4
PAST_LEN = 2048
PAGE_SIZE = 128

N_A = DEPTH // 2
N_B = DEPTH - N_A
MEM_LEN = 256
MEM_HEADS = 4
MEM_W = D_MODEL // 4
MEM_HD = MEM_W // MEM_HEADS
C_A = D_MODEL - MEM_W
CONV_A_WIDTH = 31
DIFF_HD = 64
DIFF_VD = 2 * DIFF_HD
DIFF_HEADS = (D_MODEL - MEM_W) // DIFF_VD
DIFF_QW = DIFF_HEADS * 2 * DIFF_HD
DIFF_KVW = DIFF_HEADS * DIFF_VD
FFN_CONV_WIDTH = 3
D_FF = ((8 * D_MODEL // 3 + 255) // 256) * 256
Q_BLOCK = 128
EPS = 1e-6
LN_EPS = 1e-5
NEG_INF = -1e30

kernel_name = 'yoco_conformer_diffattn_memxattn_step'


def _rms(x, g):
    xf = x.astype(jnp.float32)
    y = xf * lax.rsqrt(jnp.mean(xf * xf, axis=-1, keepdims=True) + EPS)
    return (y * g.astype(jnp.float32)).astype(x.dtype)


def _layernorm(x, g, b):
    xf = x.astype(jnp.float32)
    xc = xf - jnp.mean(xf, axis=-1, keepdims=True)
    var = jnp.mean(xc * xc, axis=-1, keepdims=True)
    y = xc * lax.rsqrt(var + LN_EPS) * g.astype(jnp.float32) + b.astype(jnp.float32)
    return y.astype(x.dtype)


def _causal_dwconv(x, prev, w, b):
    ext = jnp.concatenate([prev.astype(x.dtype), x], axis=1)
    y = lax.conv_general_dilated(ext, w[:, None, :].astype(x.dtype), (1,), 'VALID',
                                 dimension_numbers=('NWC', 'WIO', 'NWC'),
                                 feature_group_count=x.shape[-1])
    return y + b, ext[:, ext.shape[1] - (w.shape[0] - 1):]


def _mem_kv(mem, g, w, k_g):
    b, m, _ = mem.shape
    k, v = jnp.split(_rms(mem, g) @ w, 2, axis=-1)
    k = _rms(k.reshape(b, m, MEM_HEADS, MEM_HD), k_g)
    return k, v.reshape(b, m, MEM_HEADS, MEM_HD)


def _mem_attend(q, k, v):
    s = jnp.einsum('bthd,bmhd->bhtm', q, k).astype(jnp.float32) * (MEM_HD ** -0.5)
    p = jax.nn.softmax(s, axis=-1).astype(v.dtype)
    o = jnp.einsum('bhtm,bmhd->bthd', p, v)
    return o.reshape(q.shape[0], q.shape[1], MEM_W)


def _shared_kv(x, g, w, k_g):
    b, t, _ = x.shape
    k, v = jnp.split(_rms(x, g) @ w, [DIFF_QW], axis=-1)
    k = _rms(k.reshape(b, t, DIFF_HEADS, 2, DIFF_HD), k_g).reshape(b, t, DIFF_HEADS, DIFF_VD)
    return k, v.reshape(b, t, DIFF_HEADS, DIFF_VD)


def _lambda(lams, lam_init):
    lf = lams.astype(jnp.float32)
    return jnp.exp(jnp.sum(lf[0] * lf[1])) - jnp.exp(jnp.sum(lf[2] * lf[3])) + lam_init


def _diff_core(q, k, v, mask, lam):
    s = jnp.einsum('bqhcd,bkhcd->bhcqk', q, k).astype(jnp.float32) * (DIFF_HD ** -0.5)
    p = jax.nn.softmax(jnp.where(mask, s, NEG_INF), axis=-1)
    a = (p[:, :, 0] - lam * p[:, :, 1]).astype(v.dtype)
    return jnp.einsum('bhqk,bkhe->bqhe', a, v)


def _prompt_attend(q, k, v, lam):
    b, s = q.shape[:2]
    nb = s // Q_BLOCK
    kk = k.reshape(b, s, DIFF_HEADS, 2, DIFF_HD)
    qb = jnp.moveaxis(q.reshape(b, nb, Q_BLOCK, DIFF_HEADS, 2, DIFF_HD), 1, 0)
    kpos = jnp.arange(s)

    def block(args):
        q_i, i = args
        qpos = i * Q_BLOCK + jnp.arange(Q_BLOCK)
        return _diff_core(q_i, kk, v, kpos[None, :] <= qpos[:, None], lam)

    o = lax.map(block, (qb, jnp.arange(nb)))
    return jnp.moveaxis(o, 0, 1).reshape(b, s, DIFF_HEADS, DIFF_VD)


def _sample_attend(q, k, v, lam, past_k, past_v):
    b, t = q.shape[:2]
    p_len = past_k.shape[1]
    k_all = jnp.concatenate([past_k.astype(k.dtype), k], axis=1).reshape(b, p_len + t, DIFF_HEADS, 2, DIFF_HD)
    v_all = jnp.concatenate([past_v.astype(v.dtype), v], axis=1)
    qpos = p_len + jnp.arange(t)
    kpos = jnp.arange(p_len + t)
    return _diff_core(q, k_all, v_all, kpos[None, :] <= qpos[:, None], lam)


def _trunk(x, mem_k, mem_v, conv_a_prev, ffn_prev, attend, P):
    b, t, _ = x.shape
    new_conv_a, new_ffn = [], []
    k = v = None
    for l in range(DEPTH):
        h = _rms(x, P['norm_mix'][l])
        if l < N_A:
            a, g, mq = jnp.split(h @ P['w_in_a'][l], [C_A, 2 * C_A], axis=-1)
            c, st = _causal_dwconv(a * jax.nn.sigmoid(g), conv_a_prev[l], P['conv_a_w'][l], P['conv_a_b'][l])
            new_conv_a.append(st)
            c = jax.nn.silu(_layernorm(c, P['ln_a_g'][l], P['ln_a_b'][l]))
            w_out = P['w_out_a'][l]
        else:
            j = l - N_A
            if j == 0:
                k, v = _shared_kv(x, P['norm_kv'], P['w_kv'], P['k_norm'])
            qf, mq = jnp.split(h @ P['w_in_b'][j], [DIFF_QW], axis=-1)
            q = _rms(qf.reshape(b, t, DIFF_HEADS, 2, DIFF_HD), P['q_norm'][j])
            lam_init = 0.8 - 0.6 * math.exp(-0.3 * l)
            o = attend(q, k, v, _lambda(P['lambdas'][j], lam_init))
            c = (_rms(o, P['subln_g'][j]) * (1.0 - lam_init)).reshape(b, t, DIFF_KVW)
            w_out = P['w_out_b'][j]
        mq = _rms(mq.reshape(b, t, MEM_HEADS, MEM_HD), P['mem_q_norm'][l])
        m = _mem_attend(mq, mem_k[l], mem_v[l])
        x = x + jnp.concatenate([c, m], axis=-1) @ w_out
        h = _rms(x, P['norm_ffn'][l])
        gate, val = jnp.split(h @ P['w_up'][l], 2, axis=-1)
        gate, st = _causal_dwconv(gate, ffn_prev[l], P['ffn_conv_w'][l], P['ffn_conv_b'][l])
        new_ffn.append(st)
        x = x + (jax.nn.gelu(gate) * val) @ P['w_down'][l]
    return x, jnp.stack(new_conv_a), jnp.stack(new_ffn), k, v


def setup_inputs(seed: int = 0) -> dict:
    key = jax.random.key(seed)
    ks = iter(jax.random.split(key, 48))

    def nrm(shape, scale=1.0):
        return jax.random.normal(next(ks), shape, jnp.float32) * scale

    def gain(shape):
        return 1.0 + nrm(shape, 0.02)

    n_pages = PAST_LEN // PAGE_SIZE
    n_used = DEC_BATCH * n_pages
    n_phys = n_used + (n_used + 3) // 4
    page_table = jax.random.permutation(next(ks), n_phys)[:n_used].reshape(DEC_BATCH, n_pages).astype(jnp.int32)
    return {
        'x_prompt': nrm((BATCH, SEQ, D_MODEL)),
        'x_sample': nrm((DEC_BATCH, DEC_SEQ, D_MODEL)),
        'mem_prompt': nrm((BATCH, MEM_LEN, D_MODEL)),
        'cache_mem_k': nrm((DEPTH, DEC_BATCH, MEM_LEN, MEM_HEADS, MEM_HD)),
        'cache_mem_v': nrm((DEPTH, DEC_BATCH, MEM_LEN, MEM_HEADS, MEM_HD)),
        'state_conv_a': nrm((N_A, DEC_BATCH, CONV_A_WIDTH - 1, C_A), 0.5),
        'state_ffn_conv': nrm((DEPTH, DEC_BATCH, FFN_CONV_WIDTH - 1, D_FF)),
        'cache_k': nrm((n_phys, PAGE_SIZE, DIFF_HEADS, DIFF_VD)),
        'cache_v': nrm((n_phys, PAGE_SIZE, DIFF_HEADS, DIFF_VD)),
        'page_table': page_table,
        'norm_mix': gain((DEPTH, D_MODEL)),
        'norm_ffn': gain((DEPTH, D_MODEL)),
        'norm_mem': gain((DEPTH, D_MODEL)),
        'w_mem_kv': nrm((DEPTH, D_MODEL, 2 * MEM_W), D_MODEL ** -0.5),
        'mem_q_norm': gain((DEPTH, MEM_HD)),
        'mem_k_norm': gain((DEPTH, MEM_HD)),
        'w_in_a': nrm((N_A, D_MODEL, 2 * C_A + MEM_W), D_MODEL ** -0.5),
        'conv_a_w': nrm((N_A, CONV_A_WIDTH, C_A), CONV_A_WIDTH ** -0.5),
        'conv_a_b': nrm((N_A, C_A), 0.01),
        'ln_a_g': gain((N_A, C_A)),
        'ln_a_b': nrm((N_A, C_A), 0.01),
        'w_out_a': nrm((N_A, C_A + MEM_W, D_MODEL), (C_A + MEM_W) ** -0.5),
        'norm_kv': gain((D_MODEL,)),
        'w_kv': nrm((D_MODEL, DIFF_QW + DIFF_KVW), D_MODEL ** -0.5),
        'k_norm': gain((2, DIFF_HD)),
        'w_in_b': nrm((N_B, D_MODEL, DIFF_QW + MEM_W), D_MODEL ** -0.5),
        'q_norm': gain((N_B, 2, DIFF_HD)),
        'lambdas': nrm((N_B, 4, DIFF_HD), 0.1),
        'subln_g': gain((N_B, DIFF_VD)),
        'w_out_b': nrm((N_B, DIFF_KVW + MEM_W, D_MODEL), (DIFF_KVW + MEM_W) ** -0.5),
        'w_up': nrm((DEPTH, D_MODEL, 2 * D_FF), D_MODEL ** -0.5),
        'ffn_conv_w': nrm((DEPTH, FFN_CONV_WIDTH, D_FF), FFN_CONV_WIDTH ** -0.5),
        'ffn_conv_b': nrm((DEPTH, D_FF), 0.01),
        'w_down': nrm((DEPTH, D_FF, D_MODEL), D_FF ** -0.5),
    }


def reference(x_prompt, x_sample, mem_prompt, cache_mem_k, cache_mem_v, state_conv_a, state_ffn_conv,
              cache_k, cache_v, page_table, norm_mix, norm_ffn, norm_mem, w_mem_kv, mem_q_norm, mem_k_norm,
              w_in_a, conv_a_w, conv_a_b, ln_a_g, ln_a_b, w_out_a, norm_kv, w_kv, k_norm, w_in_b, q_norm,
              lambdas, subln_g, w_out_b, w_up, ffn_conv_w, ffn_conv_b, w_down):
    P = {'norm_mix': norm_mix, 'norm_ffn': norm_ffn, 'mem_q_norm': mem_q_norm,
         'w_in_a': w_in_a, 'conv_a_w': conv_a_w, 'conv_a_b': conv_a_b, 'ln_a_g': ln_a_g, 'ln_a_b': ln_a_b,
         'w_out_a': w_out_a, 'norm_kv': norm_kv, 'w_kv': w_kv, 'k_norm': k_norm, 'w_in_b': w_in_b,
         'q_norm': q_norm, 'lambdas': lambdas, 'subln_g': subln_g, 'w_out_b': w_out_b, 'w_up': w_up,
         'ffn_conv_w': ffn_conv_w, 'ffn_conv_b': ffn_conv_b, 'w_down': w_down}

    mk, mv = [], []
    for l in range(DEPTH):
        k_l, v_l = _mem_kv(mem_prompt, norm_mem[l], w_mem_kv[l], mem_k_norm[l])
        mk.append(k_l)
        mv.append(v_l)
    p_mem_k = jnp.stack(mk)
    p_mem_v = jnp.stack(mv)
    b_p = x_prompt.shape[0]
    zeros_a = jnp.zeros((N_A, b_p, CONV_A_WIDTH - 1, C_A), x_prompt.dtype)
    zeros_f = jnp.zeros((DEPTH, b_p, FFN_CONV_WIDTH - 1, D_FF), x_prompt.dtype)
    y_prompt, p_conv_a, p_ffn_conv, p_k, p_v = _trunk(x_prompt, p_mem_k, p_mem_v, zeros_a, zeros_f,
                                                       _prompt_attend, P)

    n_seq, n_pages = page_table.shape
    past_k = cache_k[page_table].reshape(n_seq, n_pages * PAGE_SIZE, DIFF_HEADS, DIFF_VD)
    past_v = cache_v[page_table].reshape(n_seq, n_pages * PAGE_SIZE, DIFF_HEADS, DIFF_VD)
    attend = functools.partial(_sample_attend, past_k=past_k, past_v=past_v)
    y_sample, s_conv_a, s_ffn_conv, s_k, s_v = _trunk(x_sample, cache_mem_k, cache_mem_v, state_conv_a,
                                                       state_ffn_conv, attend, P)
    return (y_prompt, y_sample, p_mem_k, p_mem_v, p_conv_a, p_ffn_conv, p_k, p_v,
            s_conv_a, s_ffn_conv, s_k, s_v)
```

```python
import functools
import math

import jax
import jax.numpy as jnp
from jax import lax
from jax.experimental import pallas as pl
from jax.experimental.pallas import tpu as pltpu

F32 = jnp.float32
BF16 = jnp.bfloat16

D_MODEL = 1024
MEM_LEN = 256
MEM_HEADS = 4
MEM_W = 256
MEM_HD = 64
C_A = 768
CONV_A_WIDTH = 31
DIFF_HD = 64
DIFF_VD = 128
DIFF_HEADS = 6
DIFF_W = 768
D_FF = 2816
PAGE_SIZE = 128
EPS = 1e-6
LN_EPS = 1e-5
NEG_INF = -1e30
LAM_INIT_B = 0.8 - 0.6 * math.exp(-0.3 * 1)

FF_CHUNK = 256
N_FF_CHUNKS = D_FF // FF_CHUNK
ROW_TILE = 512
CONV_ROWS = 128
CONV_LANES = 256
CARRY_ROWS = 32
PAD_T = 8
VMEM_LIMIT = 56 * 1024 * 1024


def _const_spec(shape):
    n = len(shape)
    return pl.BlockSpec(shape, lambda *_: (0,) * n, pipeline_mode=pl.Buffered(1))


def _params(n_grid):
    return pltpu.CompilerParams(dimension_semantics=("arbitrary",) * n_grid,
                                vmem_limit_bytes=VMEM_LIMIT)


def _dot(a, b):
    return jnp.dot(a, b, preferred_element_type=F32)


def _dot_nt(a, b):
    return lax.dot_general(a, b, (((1,), (1,)), ((), ())), preferred_element_type=F32)


def _rms(x, g):
    return x * lax.rsqrt(jnp.mean(x * x, axis=-1, keepdims=True) + EPS) * g


def _group_rms(x, g, group):
    width = x.shape[-1]
    blk = 256
    r = lax.broadcasted_iota(jnp.int32, (blk, blk), 0)
    c = lax.broadcasted_iota(jnp.int32, (blk, blk), 1)
    shift = group.bit_length() - 1
    ones = (jnp.right_shift(r, shift) == jnp.right_shift(c, shift)).astype(BF16)
    outs = []
    for i in range(width // blk):
        xc = x[:, i * blk:(i + 1) * blk]
        ss = _dot((xc * xc).astype(BF16), ones)
        outs.append(xc * lax.rsqrt(ss * (1.0 / group) + EPS) * g[:, i * blk:(i + 1) * blk])
    return outs[0] if len(outs) == 1 else jnp.concatenate(outs, axis=-1)


def _layernorm(x, g, b):
    xc = x - jnp.mean(x, axis=-1, keepdims=True)
    var = jnp.mean(xc * xc, axis=-1, keepdims=True)
    return xc * lax.rsqrt(var + LN_EPS) * g + b


def _silu(x):
    return x * jax.nn.sigmoid(x)


def _lane_group(shape, shift):
    return jnp.right_shift(lax.broadcasted_iota(jnp.int32, shape, len(shape) - 1), shift)


def _mem_attend_shared(mq, mk, mv):
    head_of_lane = _lane_group(mq.shape, 6)
    out = jnp.zeros(mq.shape, F32)
    for h in range(MEM_HEADS):
        sel = head_of_lane == h
        s = _dot_nt(jnp.where(sel, mq, 0.0).astype(BF16), mk)
        p = jnp.exp(s - jnp.max(s, axis=-1, keepdims=True))
        o = _dot(p.astype(BF16), mv) / jnp.sum(p, axis=-1, keepdims=True)
        out = jnp.where(sel, o, out)
    return out


def _lambda(lf):
    s1 = jnp.sum(lf[0:1] * lf[1:2], axis=-1, keepdims=True)
    s2 = jnp.sum(lf[2:3] * lf[3:4], axis=-1, keepdims=True)
    return jnp.exp(s1) - jnp.exp(s2) + LAM_INIT_B


def _mem_kv_kernel(mem_ref, g_ref, w_ref, kg_ref, k_ref, v_ref):
    h = _rms(mem_ref[0], g_ref[0]).astype(BF16)
    kv = _dot(h, w_ref[0])
    k_ref[0, 0] = _group_rms(kv[:, :MEM_W], kg_ref[0], MEM_HD)
    v_ref[0, 0] = kv[:, MEM_W:]


def _mem_kv(mem, norm_mem, w_mem_kv, mem_k_gain):
    depth, batch = w_mem_kv.shape[0], mem.shape[0]
    out = jax.ShapeDtypeStruct((depth, batch, MEM_LEN, MEM_W), F32)
    return pl.pallas_call(
        _mem_kv_kernel,
        grid=(depth, batch),
        in_specs=[
            pl.BlockSpec((1, MEM_LEN, D_MODEL), lambda l, b: (b, 0, 0)),
            pl.BlockSpec((1, 1, D_MODEL), lambda l, b: (l, 0, 0)),
            pl.BlockSpec((1, D_MODEL, 2 * MEM_W), lambda l, b: (l, 0, 0)),
            pl.BlockSpec((1, 1, MEM_W), lambda l, b: (l, 0, 0)),
        ],
        out_specs=[pl.BlockSpec((1, 1, MEM_LEN, MEM_W), lambda l, b: (l, b, 0, 0))] * 2,
        out_shape=[out, out],
        compiler_params=_params(2),
        name="mem_kv",
    )(mem, norm_mem, w_mem_kv, mem_k_gain)


def _conv31_tile(ext_ref, cw_ref, cb_ref, r0, lanes):
    acc = jnp.broadcast_to(cb_ref[:, lanes], (CONV_ROWS, CONV_LANES))
    for r in range(8):
        offs = [o for o in range(2, CONV_A_WIDTH + 2) if o % 8 == r]
        span = CONV_ROWS + 8 * (max(offs) // 8)
        u = ext_ref[r0 + r:r0 + r + span, lanes]
        for o in offs:
            a = o // 8
            acc = acc + cw_ref[o - 2:o - 1, lanes] * u[8 * a:8 * a + CONV_ROWS]
    return acc


def _p_mixa_kernel(x_ref, gmix_ref, win_ref, cw_ref, cb_ref, lng_ref, lnb_ref, mqg_ref, mk_ref, mv_ref,
                   wout_ref, x1_ref, st_ref, ext_ref, cm_ref):
    tile = x_ref.shape[1]

    @pl.when(pl.program_id(1) == 0)
    def _():
        ext_ref[0:CARRY_ROWS, :] = jnp.zeros((CARRY_ROWS, C_A), F32)

    x = x_ref[0]
    h = _rms(x, gmix_ref[...]).astype(BF16)
    z = _dot(h, win_ref[...])
    ext_ref[CARRY_ROWS:CARRY_ROWS + tile, :] = z[:, :C_A] * jax.nn.sigmoid(z[:, C_A:2 * C_A])

    for r0 in range(0, tile, CONV_ROWS):
        y = jnp.concatenate(
            [_conv31_tile(ext_ref, cw_ref, cb_ref, r0, slice(l0, l0 + CONV_LANES))
             for l0 in range(0, C_A, CONV_LANES)], axis=-1)
        c = _silu(_layernorm(y, lng_ref[...], lnb_ref[...]))
        cm_ref[r0:r0 + CONV_ROWS, 0:C_A] = c.astype(BF16)

    st_ref[0] = ext_ref[tile:tile + CARRY_ROWS, :]
    ext_ref[0:CARRY_ROWS, :] = ext_ref[tile:tile + CARRY_ROWS, :]

    mq = _group_rms(z[:, 2 * C_A:], mqg_ref[...], MEM_HD) * (MEM_HD ** -0.5)
    m = _mem_attend_shared(mq, mk_ref[0, 0].astype(BF16), mv_ref[0, 0].astype(BF16))
    cm_ref[:, C_A:] = m.astype(BF16)
    x1_ref[0] = x + _dot(cm_ref[...], wout_ref[...])


def _p_mixa(x, gmix, win, cw, cb, lng, lnb, mqg, mem_k, mem_v, wout):
    batch, seq, _ = x.shape
    tile = ROW_TILE
    tok = pl.BlockSpec((1, tile, D_MODEL), lambda b, t: (b, t, 0))
    mem = pl.BlockSpec((1, 1, MEM_LEN, MEM_W), lambda b, t: (0, b, 0, 0))
    return pl.pallas_call(
        _p_mixa_kernel,
        grid=(batch, seq // tile),
        in_specs=[tok, _const_spec(gmix.shape), _const_spec(win.shape), _const_spec(cw.shape),
                  _const_spec(cb.shape), _const_spec(lng.shape), _const_spec(lnb.shape),
                  _const_spec(mqg.shape), mem, mem, _const_spec(wout.shape)],
        out_specs=[tok, pl.BlockSpec((1, CARRY_ROWS, C_A), lambda b, t: (b, 0, 0))],
        out_shape=[jax.ShapeDtypeStruct(x.shape, F32),
                   jax.ShapeDtypeStruct((batch, CARRY_ROWS, C_A), F32)],
        scratch_shapes=[pltpu.VMEM((CARRY_ROWS + tile, C_A), F32),
                        pltpu.VMEM((tile, D_MODEL), BF16)],
        compiler_params=_params(2),
        name="p_mixa",
    )(x, gmix, win, cw, cb, lng, lnb, mqg, mem_k, mem_v, wout)


def _ffn_chunks(h, wup_ref, cw_ref, cb_ref, wdn_ref, conv_fn, acc):
    for j in range(N_FF_CHUNKS):
        lanes = slice(j * FF_CHUNK, (j + 1) * FF_CHUNK)
        gate = _dot(h, wup_ref[:, lanes])
        val = _dot(h, wup_ref[:, D_FF + j * FF_CHUNK:D_FF + (j + 1) * FF_CHUNK])
        g2, g1 = conv_fn(lanes, gate)
        conv = (cw_ref[0:1, lanes] * g2 + cw_ref[1:2, lanes] * g1 + cw_ref[2:3, lanes] * gate
                + cb_ref[:, lanes])
        act = (jax.nn.gelu(conv) * val).astype(BF16)
        acc = acc + _dot(act, wdn_ref[lanes, :])
    return acc


def _p_ffn_kernel(x_ref, g_ref, wup_ref, cw_ref, cb_ref, wdn_ref, o_ref, st_ref, carry_ref):
    tile = x_ref.shape[1]

    @pl.when(pl.program_id(1) == 0)
    def _():
        carry_ref[...] = jnp.zeros(carry_ref.shape, F32)

    x = x_ref[0]
    h = _rms(x, g_ref[...]).astype(BF16)
    row = lax.broadcasted_iota(jnp.int32, (tile, FF_CHUNK), 0)

    def shifted(lanes, gate):
        c2 = carry_ref[0:1, lanes]
        c1 = carry_ref[1:2, lanes]
        g1 = jnp.where(row == 0, c1, pltpu.roll(gate, 1, 0))
        g2 = jnp.where(row == 0, c2, jnp.where(row == 1, c1, pltpu.roll(gate, 2, 0)))
        carry_ref[0:2, lanes] = gate[tile - 2:tile]
        return g2, g1

    o_ref[0] = _ffn_chunks(h, wup_ref, cw_ref, cb_ref, wdn_ref, shifted, x)
    st_ref[0] = carry_ref[0:2, :]


def _p_ffn(x, g, wup, cw, cb, wdn):
    batch, seq, _ = x.shape
    tile = ROW_TILE
    tok = pl.BlockSpec((1, tile, D_MODEL), lambda b, t: (b, t, 0))
    return pl.pallas_call(
        _p_ffn_kernel,
        grid=(batch, seq // tile),
        in_specs=[tok, _const_spec(g.shape), _const_spec(wup.shape), _const_spec(cw.shape),
                  _const_spec(cb.shape), _const_spec(wdn.shape)],
        out_specs=[tok, pl.BlockSpec((1, 2, D_FF), lambda b, t: (b, 0, 0))],
        out_shape=[jax.ShapeDtypeStruct(x.shape, F32), jax.ShapeDtypeStruct((batch, 2, D_FF), F32)],
        scratch_shapes=[pltpu.VMEM((8, D_FF), F32)],
        compiler_params=_params(2),
        name="p_ffn",
    )(x, g, wup, cw, cb, wdn)


def _s_ffn_kernel(x_ref, c_ref, m_ref, wout_ref, g_ref, wup_ref, cw_ref, cb_ref, wdn_ref, st_ref,
                  o_ref, nst_ref):
    n_seq = st_ref.shape[1]
    o_ref[...] = (x_ref[...] + _dot(c_ref[...], wout_ref[0:DIFF_W, :])
                  + _dot(m_ref[...], wout_ref[DIFF_W:, :]))
    h = _rms(o_ref[...], g_ref[...]).astype(BF16)

    def shifted(lanes, gate):
        ext = [st_ref[0, :, lanes], st_ref[1, :, lanes]] + [
            gate[t * n_seq:(t + 1) * n_seq] for t in range(gate.shape[0] // n_seq)]
        nst_ref[0, :, lanes] = ext[-2]
        nst_ref[1, :, lanes] = ext[-1]
        return jnp.concatenate(ext[:-2], axis=0), jnp.concatenate(ext[1:-1], axis=0)

    o_ref[...] = _ffn_chunks(h, wup_ref, cw_ref, cb_ref, wdn_ref, shifted, o_ref[...])


def _s_ffn(x, c, m, wout, g, wup, cw, cb, wdn, st):
    args = (x, c, m, wout, g, wup, cw, cb, wdn, st)
    return pl.pallas_call(
        _s_ffn_kernel,
        grid=(1,),
        in_specs=[_const_spec(a.shape) for a in args],
        out_specs=[_const_spec(x.shape), _const_spec(st.shape)],
        out_shape=[jax.ShapeDtypeStruct(x.shape, F32), jax.ShapeDtypeStruct(st.shape, F32)],
        compiler_params=_params(1),
        name="s_ffn",
    )(*args)


def _mixb_project(x, gkv, wkv, kg, gmix, winb, qg, mqg):
    xn = x * lax.rsqrt(jnp.mean(x * x, axis=-1, keepdims=True) + EPS)
    kv = _dot((xn * gkv).astype(BF16), wkv)
    qm = _dot((xn * gmix).astype(BF16), winb)
    k = _group_rms(kv[:, :DIFF_W], kg, DIFF_HD)
    v = kv[:, DIFF_W:]
    q = _group_rms(qm[:, :DIFF_W], qg, DIFF_HD) * (DIFF_HD ** -0.5)
    mq = _group_rms(qm[:, DIFF_W:], mqg, MEM_HD) * (MEM_HD ** -0.5)
    return k, v, q, mq


def _p_mixb_kernel(x_ref, gkv_ref, wkv_ref, kg_ref, gmix_ref, winb_ref, qg_ref, mqg_ref, mk_ref, mv_ref,
                   k_ref, v_ref, kb_ref, vb_ref, qb_ref, mb_ref):
    k, v, q, mq = _mixb_project(x_ref[0], gkv_ref[...], wkv_ref[...], kg_ref[...], gmix_ref[...],
                                winb_ref[...], qg_ref[...], mqg_ref[...])
    k_ref[0] = k
    v_ref[0] = v
    kb_ref[0] = k.astype(BF16)
    vb_ref[0] = v.astype(BF16)
    qb_ref[0] = q.astype(BF16)
    m = _mem_attend_shared(mq, mk_ref[0, 0].astype(BF16), mv_ref[0, 0].astype(BF16))
    mb_ref[0] = m.astype(BF16)


def _p_mixb(x, gkv, wkv, kg, gmix, winb, qg, mqg, mem_k, mem_v):
    batch, seq, _ = x.shape
    tile = ROW_TILE
    tok = lambda w: pl.BlockSpec((1, tile, w), lambda b, t: (b, t, 0))
    mem = pl.BlockSpec((1, 1, MEM_LEN, MEM_W), lambda b, t: (1, b, 0, 0))
    shp = lambda w, dt: jax.ShapeDtypeStruct((batch, seq, w), dt)
    return pl.pallas_call(
        _p_mixb_kernel,
        grid=(batch, seq // tile),
        in_specs=[tok(D_MODEL)] + [_const_spec(a.shape) for a in (gkv, wkv, kg, gmix, winb, qg, mqg)]
        + [mem, mem],
        out_specs=[tok(DIFF_W)] * 5 + [tok(MEM_W)],
        out_shape=[shp(DIFF_W, F32), shp(DIFF_W, F32), shp(DIFF_W, BF16), shp(DIFF_W, BF16),
                   shp(DIFF_W, BF16), shp(MEM_W, BF16)],
        compiler_params=_params(2),
        name="p_mixb",
    )(x, gkv, wkv, kg, gmix, winb, qg, mqg, mem_k, mem_v)


def _s_mixb_kernel(x_ref, gkv_ref, wkv_ref, kg_ref, gmix_ref, winb_ref, qg_ref, mqg_ref,
                   k_ref, v_ref, q_ref, mq_ref):
    k, v, q, mq = _mixb_project(x_ref[...], gkv_ref[...], wkv_ref[...], kg_ref[...], gmix_ref[...],
                                winb_ref[...], qg_ref[...], mqg_ref[...])
    k_ref[...] = k
    v_ref[...] = v
    q_ref[...] = q
    mq_ref[...] = mq


def _s_mixb(x, gkv, wkv, kg, gmix, winb, qg, mqg):
    n = x.shape[0]
    args = (x, gkv, wkv, kg, gmix, winb, qg, mqg)
    widths = (DIFF_W, DIFF_W, DIFF_W, MEM_W)
    return pl.pallas_call(
        _s_mixb_kernel,
        grid=(1,),
        in_specs=[_const_spec(a.shape) for a in args],
        out_specs=[_const_spec((n, w)) for w in widths],
        out_shape=[jax.ShapeDtypeStruct((n, w), F32) for w in widths],
        compiler_params=_params(1),
        name="s_mixb",
    )(*args)


def _p_att_kernel(q_ref, k_ref, v_ref, m_ref, x_ref, lam_ref, sg_ref, wout_ref, o_ref, c_ref):
    tq = q_ref.shape[1]
    qi = pl.program_id(1)
    lam = _lambda(lam_ref[...])
    row = lax.broadcasted_iota(jnp.int32, (tq, tq), 0)
    col = lax.broadcasted_iota(jnp.int32, (tq, tq), 1)
    causal = col <= row
    first_map = lax.broadcasted_iota(jnp.int32, (tq, DIFF_VD), 1) < DIFF_HD

    for h in range(DIFF_HEADS):
        lanes = slice(h * DIFF_VD, (h + 1) * DIFF_VD)
        q = q_ref[0, :, lanes]
        zero = jnp.zeros_like(q)
        qs = (jnp.where(first_map, q, zero), jnp.where(first_map, zero, q))

        def block(j, carry, masked):
            start = pl.multiple_of(j * tq, tq)
            kb = k_ref[0, pl.ds(start, tq), lanes]
            vb = v_ref[0, pl.ds(start, tq), lanes]
            new = []
            for c in range(2):
                m_old, l_old, acc = carry[c]
                s = _dot_nt(qs[c], kb)
                if masked:
                    s = jnp.where(causal, s, NEG_INF)
                m_new = jnp.maximum(m_old, jnp.max(s, axis=-1, keepdims=True))
                alpha = jnp.exp(m_old - m_new)
                p = jnp.exp(s - m_new)
                new.append((m_new, alpha * l_old + jnp.sum(p, axis=-1, keepdims=True),
                            alpha * acc + _dot(p.astype(BF16), vb)))
            return tuple(new)

        init = tuple((jnp.full((tq, 1), NEG_INF, F32), jnp.zeros((tq, 1), F32),
                      jnp.zeros((tq, DIFF_VD), F32)) for _ in range(2))
        carry = lax.fori_loop(0, qi, lambda j, c: block(j, c, False), init)
        (_, l1, a1), (_, l2, a2) = block(qi, carry, True)
        o = a1 / l1 - lam * (a2 / l2)
        o = _rms(o, sg_ref[:, lanes]) * (1.0 - LAM_INIT_B)
        c_ref[:, lanes] = o.astype(BF16)

    o_ref[0] = (x_ref[0] + _dot(c_ref[...], wout_ref[0:DIFF_W, :])
                + _dot(m_ref[0], wout_ref[DIFF_W:, :]))


def _p_att(q, k, v, m, x, lam, sg, wout):
    batch, seq, _ = x.shape
    tile = ROW_TILE
    tok = lambda w: pl.BlockSpec((1, tile, w), lambda b, t: (b, t, 0))
    kv = pl.BlockSpec((1, seq, DIFF_W), lambda b, t: (b, 0, 0), pipeline_mode=pl.Buffered(1))
    return pl.pallas_call(
        _p_att_kernel,
        grid=(batch, seq // tile),
        in_specs=[tok(DIFF_W), kv, kv, tok(MEM_W), tok(D_MODEL), _const_spec(lam.shape),
                  _const_spec(sg.shape), _const_spec(wout.shape)],
        out_specs=tok(D_MODEL),
        out_shape=jax.ShapeDtypeStruct(x.shape, F32),
        scratch_shapes=[pltpu.VMEM((tile, DIFF_W), BF16)],
        compiler_params=_params(2),
        name="p_att",
    )(q, k, v, m, x, lam, sg, wout)


def _s_mixa_kernel(x_ref, st_ref, gmix_ref, win_ref, cw_ref, cb_ref, lng_ref, lnb_ref, mqg_ref,
                   glu_ref, c_ref, mq_ref):
    n_t, nb, _ = x_ref.shape
    n_hist = st_ref.shape[0]
    x = x_ref[...].reshape(n_t * nb, D_MODEL)
    h = _rms(x, gmix_ref[...]).astype(BF16)
    z = _dot(h, win_ref[...])
    glu = z[:, :C_A] * jax.nn.sigmoid(z[:, C_A:2 * C_A])
    glu_ref[...] = glu.reshape(n_t, nb, C_A)
    mq = _group_rms(z[:, 2 * C_A:], mqg_ref[...], MEM_HD) * (MEM_HD ** -0.5)
    mq_ref[...] = mq.reshape(n_t, nb, MEM_W)

    ys = [[] for _ in range(n_t)]
    for l0 in range(0, C_A, CONV_LANES):
        lanes = slice(l0, l0 + CONV_LANES)
        acc = [jnp.broadcast_to(cb_ref[:, lanes], (nb, CONV_LANES)) for _ in range(n_t)]
        for r in range(n_hist + n_t):
            e = st_ref[r, :, lanes] if r < n_hist else glu[(r - n_hist) * nb:(r - n_hist + 1) * nb, lanes]
            for t in range(n_t):
                if 0 <= r - t < CONV_A_WIDTH:
                    acc[t] = acc[t] + cw_ref[r - t:r - t + 1, lanes] * e
        for t in range(n_t):
            ys[t].append(acc[t])
    for t in range(n_t):
        y = jnp.concatenate(ys[t], axis=-1)
        c_ref[t] = _silu(_layernorm(y, lng_ref[...], lnb_ref[...])).astype(BF16)


def _s_mixa(x_tm, st_tm, gmix, win, cw, cb, lng, lnb, mqg):
    n_t, n_seq, _ = x_tm.shape
    nb = 32
    blk = lambda r, w: pl.BlockSpec((r, nb, w), lambda i: (0, i, 0))
    consts = (gmix, win, cw, cb, lng, lnb, mqg)
    return pl.pallas_call(
        _s_mixa_kernel,
        grid=(n_seq // nb,),
        in_specs=[blk(n_t, D_MODEL), blk(st_tm.shape[0], C_A)] + [_const_spec(a.shape) for a in consts],
        out_specs=[blk(n_t, C_A), blk(n_t, C_A), blk(n_t, MEM_W)],
        out_shape=[jax.ShapeDtypeStruct((n_t, n_seq, C_A), F32),
                   jax.ShapeDtypeStruct((n_t, n_seq, C_A), BF16),
                   jax.ShapeDtypeStruct((n_t, n_seq, MEM_W), F32)],
        compiler_params=_params(1),
        name="s_mixa",
    )(x_tm, st_tm, *consts)


def _s_mem_kernel(q_ref, mk_ref, mv_ref, o_ref):
    nb = q_ref.shape[0]
    rows = MEM_HEADS * PAD_T
    head_of_row = jnp.right_shift(lax.broadcasted_iota(jnp.int32, (rows, MEM_W), 0), 3)
    own = head_of_row == _lane_group((rows, MEM_W), 6)
    head_of_lane = _lane_group((PAD_T, MEM_W), 6)
    for b in range(nb):
        q = q_ref[b]
        qb = jnp.where(own, jnp.concatenate([q] * MEM_HEADS, axis=0), 0.0).astype(BF16)
        s = _dot_nt(qb, mk_ref[0, b].astype(BF16))
        p = jnp.exp(s - jnp.max(s, axis=-1, keepdims=True))
        o = _dot(p.astype(BF16), mv_ref[0, b].astype(BF16)) / jnp.sum(p, axis=-1, keepdims=True)
        out = jnp.zeros((PAD_T, MEM_W), F32)
        for h in range(MEM_HEADS):
            out = jnp.where(head_of_lane == h, o[h * PAD_T:(h + 1) * PAD_T], out)
        o_ref[b] = out


def _s_mem(q_pad, mem_k, mem_v, layer):
    n_seq = q_pad.shape[0]
    nb = 8
    tok = pl.BlockSpec((nb, PAD_T, MEM_W), lambda i: (i, 0, 0))
    mem = pl.BlockSpec((1, nb, MEM_LEN, MEM_W), lambda i: (layer, i, 0, 0))
    return pl.pallas_call(
        _s_mem_kernel,
        grid=(n_seq // nb,),
        in_specs=[tok, mem, mem],
        out_specs=tok,
        out_shape=jax.ShapeDtypeStruct(q_pad.shape, F32),
        compiler_params=_params(1),
        name="s_mem",
    )(q_pad, mem_k, mem_v)


def _s_paged_kernel(pt_ref, q_ref, kn_ref, vn_ref, lam_ref, sg_ref, *refs, n_pages, n_new):
    del pt_ref
    k_refs, v_refs, o_ref = refs[:n_pages], refs[n_pages:2 * n_pages], refs[2 * n_pages]
    half = DIFF_HEADS * PAD_T
    rows = 2 * half
    lam = _lambda(lam_ref[...])

    q = q_ref[0]
    grp = jnp.right_shift(lax.broadcasted_iota(jnp.int32, (rows, DIFF_W), 0), 3)
    second = (grp >= DIFF_HEADS).astype(jnp.int32)
    own = _lane_group((rows, DIFF_W), 6) == 2 * (grp - DIFF_HEADS * second) + second
    qf = jnp.where(own, jnp.concatenate([q] * (2 * DIFF_HEADS), axis=0), 0.0)
    qb = qf.astype(BF16)

    s_past = [_dot_nt(qb, k_refs[p][0].astype(BF16)) for p in range(n_pages)]
    t_q = jnp.bitwise_and(lax.broadcasted_iota(jnp.int32, (rows, 1), 0), PAD_T - 1)
    s_new = [jnp.where(t <= t_q, jnp.sum(qf * kn_ref[0, t:t + 1, :], axis=-1, keepdims=True), NEG_INF)
             for t in range(n_new)]

    m_el = s_past[0]
    for s in s_past[1:]:
        m_el = jnp.maximum(m_el, s)
    m = jnp.max(m_el, axis=-1, keepdims=True)
    for s in s_new:
        m = jnp.maximum(m, s)

    p_past = [jnp.exp(s - m) for s in s_past]
    p_new = [jnp.exp(s - m) for s in s_new]
    l_el = p_past[0]
    for p in p_past[1:]:
        l_el = l_el + p
    l = jnp.sum(l_el, axis=-1, keepdims=True)
    for p in p_new:
        l = l + p
    w1 = 1.0 / l[:half]
    w2 = lam / l[half:]

    acc = jnp.zeros((half, DIFF_W), F32)
    for i in range(n_pages):
        a = (p_past[i][:half] * w1 - p_past[i][half:] * w2).astype(BF16)
        acc = acc + _dot(a, v_refs[i][0].astype(BF16))
    for t in range(n_new):
        acc = acc + (p_new[t][:half] * w1 - p_new[t][half:] * w2) * vn_ref[0, t:t + 1, :]

    head_of_lane = _lane_group((PAD_T, DIFF_W), 7)
    out = jnp.zeros((PAD_T, DIFF_W), F32)
    for h in range(DIFF_HEADS):
        out = jnp.where(head_of_lane == h, acc[h * PAD_T:(h + 1) * PAD_T], out)
    o_ref[0] = jnp.concatenate(
        [_rms(out[:, h * DIFF_VD:(h + 1) * DIFF_VD], sg_ref[:, h * DIFF_VD:(h + 1) * DIFF_VD])
         for h in range(DIFF_HEADS)], axis=-1) * (1.0 - LAM_INIT_B)


def _s_paged(page_table, q_pad, k_new, v_new, lam, sg, cache_k, cache_v):
    n_seq, n_pages = page_table.shape
    n_new = k_new.shape[1]
    tok = lambda r: pl.BlockSpec((1, r, DIFF_W), lambda b, pt: (b, 0, 0))
    page = lambda p: pl.BlockSpec((1, PAGE_SIZE, DIFF_W), lambda b, pt: (pt[b * n_pages + p], 0, 0))
    grid_spec = pltpu.PrefetchScalarGridSpec(
        num_scalar_prefetch=1,
        grid=(n_seq,),
        in_specs=[tok(PAD_T), tok(n_new), tok(n_new), _const_spec(lam.shape), _const_spec(sg.shape)]
        + [page(p) for p in range(n_pages)] * 2,
        out_specs=tok(PAD_T),
    )
    return pl.pallas_call(
        functools.partial(_s_paged_kernel, n_pages=n_pages, n_new=n_new),
        grid_spec=grid_spec,
        out_shape=jax.ShapeDtypeStruct(q_pad.shape, F32),
        compiler_params=_params(1),
        name="s_paged",
    )(page_table.reshape(-1), q_pad, k_new, v_new, lam, sg,
      *([cache_k] * n_pages), *([cache_v] * n_pages))


def _to_time_major(a):
    return jnp.swapaxes(a, 0, 1)


def _pad_tokens(a_tm):
    a = jnp.swapaxes(a_tm, 0, 1)
    return jnp.pad(a, ((0, 0), (0, PAD_T - a.shape[1]), (0, 0)))


def _from_padded(a, n_t, dtype):
    return jnp.swapaxes(a[:, :n_t], 0, 1).astype(dtype)


def kernel(x_prompt, x_sample, mem_prompt, cache_mem_k, cache_mem_v, state_conv_a, state_ffn_conv, cache_k, cache_v, page_table, norm_mix, norm_ffn, norm_mem, w_mem_kv, mem_q_norm, mem_k_norm, w_in_a, conv_a_w, conv_a_b, ln_a_g, ln_a_b, w_out_a, norm_kv, w_kv, k_norm, w_in_b, q_norm, lambdas, subln_g, w_out_b, w_up, ffn_conv_w, ffn_conv_b, w_down):
    batch, seq, _ = x_prompt.shape
    n_seq, n_t, _ = x_sample.shape
    depth = norm_mix.shape[0]
    assert depth == 2 and w_in_a.shape[0] == 1 and w_in_b.shape[0] == 1
    assert seq % ROW_TILE == 0 and n_t <= PAD_T and n_seq % 32 == 0

    row = lambda a: a.reshape(1, -1)
    gmix = [row(norm_mix[l]) for l in range(depth)]
    gffn = [row(norm_ffn[l]) for l in range(depth)]
    mqg = [row(jnp.tile(mem_q_norm[l], MEM_HEADS)) for l in range(depth)]
    mkg = jnp.tile(mem_k_norm, (1, MEM_HEADS)).reshape(depth, 1, MEM_W)
    kg = row(jnp.tile(k_norm.reshape(-1), DIFF_HEADS))
    qg = row(jnp.tile(q_norm[0].reshape(-1), DIFF_HEADS))
    sg = row(jnp.tile(subln_g[0], DIFF_HEADS))
    lam = lambdas[0]
    win_a = w_in_a[0].astype(BF16)
    wout_a = w_out_a[0].astype(BF16)
    wkv = w_kv.astype(BF16)
    win_b = w_in_b[0].astype(BF16)
    wout_b = w_out_b[0].astype(BF16)
    wup = [w_up[l].astype(BF16) for l in range(depth)]
    wdn = [w_down[l].astype(BF16) for l in range(depth)]
    cw_a, cb_a = conv_a_w[0], row(conv_a_b[0])
    lng, lnb = row(ln_a_g[0]), row(ln_a_b[0])
    fcw = [ffn_conv_w[l] for l in range(depth)]
    fcb = [row(ffn_conv_b[l]) for l in range(depth)]

    p_mem_k, p_mem_v = _mem_kv(mem_prompt, norm_mem.reshape(depth, 1, D_MODEL), w_mem_kv.astype(BF16), mkg)
    x1, conv_tail = _p_mixa(x_prompt, gmix[0], win_a, cw_a, cb_a, lng, lnb, mqg[0], p_mem_k, p_mem_v, wout_a)
    x2, p_ffn0 = _p_ffn(x1, gffn[0], wup[0], fcw[0], fcb[0], wdn[0])
    p_k, p_v, kb, vb, qb, mb = _p_mixb(x2, row(norm_kv), wkv, kg, gmix[1], win_b, qg, mqg[1], p_mem_k, p_mem_v)
    x3 = _p_att(qb, kb, vb, mb, x2, lam, sg, wout_b)
    y_prompt, p_ffn1 = _p_ffn(x3, gffn[1], wup[1], fcw[1], fcb[1], wdn[1])

    hd = (MEM_HEADS, MEM_HD)
    p_mem_k = p_mem_k.reshape(depth, batch, MEM_LEN, *hd)
    p_mem_v = p_mem_v.reshape(depth, batch, MEM_LEN, *hd)
    p_conv_a = conv_tail[None, :, CARRY_ROWS - (CONV_A_WIDTH - 1):, :]
    p_ffn_conv = jnp.stack([p_ffn0, p_ffn1])
    p_k = p_k.reshape(batch, seq, DIFF_HEADS, DIFF_VD)
    p_v = p_v.reshape(batch, seq, DIFF_HEADS, DIFF_VD)

    smem_k = cache_mem_k.reshape(depth, n_seq, MEM_LEN, MEM_W)
    smem_v = cache_mem_v.reshape(depth, n_seq, MEM_LEN, MEM_W)
    xs = _to_time_major(x_sample)
    glu, c_a, mq_a = _s_mixa(xs, _to_time_major(state_conv_a[0]), gmix[0], win_a, cw_a, cb_a, lng, lnb, mqg[0])
    m_a = _from_padded(_s_mem(_pad_tokens(mq_a), smem_k, smem_v, 0), n_t, BF16)
    flat = lambda a: a.reshape(n_t * n_seq, a.shape[-1])
    xs2, s_ffn0 = _s_ffn(flat(xs), flat(c_a), flat(m_a), wout_a, gffn[0], wup[0], fcw[0], fcb[0], wdn[0],
                         _to_time_major(state_ffn_conv[0]))
    k_new, v_new, q_s, mq_b = _s_mixb(xs2, row(norm_kv), wkv, kg, gmix[1], win_b, qg, mqg[1])
    unflat = lambda a: a.reshape(n_t, n_seq, a.shape[-1])
    m_b = _from_padded(_s_mem(_pad_tokens(unflat(mq_b)), smem_k, smem_v, 1), n_t, BF16)
    k_new_sm = jnp.swapaxes(unflat(k_new), 0, 1)
    v_new_sm = jnp.swapaxes(unflat(v_new), 0, 1)
    n_phys = cache_k.shape[0]
    c_b = _s_paged(page_table, _pad_tokens(unflat(q_s)), k_new_sm, v_new_sm, lam, sg,
                   cache_k.reshape(n_phys, PAGE_SIZE, DIFF_W), cache_v.reshape(n_phys, PAGE_SIZE, DIFF_W))
    c_b = _from_padded(c_b, n_t, BF16)
    ys, s_ffn1 = _s_ffn(xs2, flat(c_b), flat(m_b), wout_b, gffn[1], wup[1], fcw[1], fcb[1], wdn[1],
                        _to_time_major(state_ffn_conv[1]))

    y_sample = jnp.swapaxes(unflat(ys), 0, 1)
    s_conv_a = jnp.concatenate([state_conv_a[0][:, n_t:], jnp.swapaxes(glu, 0, 1)], axis=1)[None]
    s_ffn_conv = jnp.stack([_to_time_major(s_ffn0), _to_time_major(s_ffn1)])
    s_k = k_new_sm.reshape(n_seq, n_t, DIFF_HEADS, DIFF_VD)
    s_v = v_new_sm.reshape(n_seq, n_t, DIFF_HEADS, DIFF_VD)
    return (y_prompt, y_sample, p_mem_k, p_mem_v, p_conv_a, p_ffn_conv, p_k, p_v,
            s_conv_a, s_ffn_conv, s_k, s_v)
```

```python
import functools
import math

import jax
import jax.numpy as jnp
from jax import lax
from jax.experimental import pallas as pl
from jax.experimental.pallas import tpu as pltpu

F32 = jnp.float32
BF16 = jnp.bfloat16

D_MODEL = 1024
MEM_LEN = 256
MEM_HEADS = 4
MEM_W = 256
MEM_HD = 64
C_A = 768
CONV_A_WIDTH = 31
DIFF_HD = 64
DIFF_VD = 128
DIFF_HEADS = 6
DIFF_W = 768
D_FF = 2816
PAGE_SIZE = 128
EPS = 1e-6
LN_EPS = 1e-5
NEG_INF = -1e30
LAM_INIT_B = 0.8 - 0.6 * math.exp(-0.3 * 1)

FF_CHUNK = 256
N_FF_CHUNKS = D_FF // FF_CHUNK
ROW_TILE = 512
CONV_ROWS = 128
CONV_LANES = 256
CARRY_ROWS = 32
PAD_T = 8
VMEM_LIMIT = 56 * 1024 * 1024


def _const_spec(shape):
    n = len(shape)
    return pl.BlockSpec(shape, lambda *_: (0,) * n, pipeline_mode=pl.Buffered(1))


def _params(n_grid):
    return pltpu.CompilerParams(dimension_semantics=("arbitrary",) * n_grid,
                                vmem_limit_bytes=VMEM_LIMIT)


def _dot(a, b):
    return jnp.dot(a, b, preferred_element_type=F32)


def _dot_nt(a, b):
    return lax.dot_general(a, b, (((1,), (1,)), ((), ())), preferred_element_type=F32)


def _rms(x, g):
    return x * lax.rsqrt(jnp.mean(x * x, axis=-1, keepdims=True) + EPS) * g


def _group_rms(x, g, group):
    width = x.shape[-1]
    blk = 256
    r = lax.broadcasted_iota(jnp.int32, (blk, blk), 0)
    c = lax.broadcasted_iota(jnp.int32, (blk, blk), 1)
    shift = group.bit_length() - 1
    ones = (jnp.right_shift(r, shift) == jnp.right_shift(c, shift)).astype(BF16)
    outs = []
    for i in range(width // blk):
        xc = x[:, i * blk:(i + 1) * blk]
        ss = _dot((xc * xc).astype(BF16), ones)
        outs.append(xc * lax.rsqrt(ss * (1.0 / group) + EPS) * g[:, i * blk:(i + 1) * blk])
    return outs[0] if len(outs) == 1 else jnp.concatenate(outs, axis=-1)


def _layernorm(x, g, b):
    xc = x - jnp.mean(x, axis=-1, keepdims=True)
    var = jnp.mean(xc * xc, axis=-1, keepdims=True)
    return xc * lax.rsqrt(var + LN_EPS) * g + b


def _silu(x):
    return x * jax.nn.sigmoid(x)


def _lane_group(shape, shift):
    return jnp.right_shift(lax.broadcasted_iota(jnp.int32, shape, len(shape) - 1), shift)


def _mem_attend_shared(mq, mk, mv):
    head_of_lane = _lane_group(mq.shape, 6)
    out = jnp.zeros(mq.shape, F32)
    for h in range(MEM_HEADS):
        sel = head_of_lane == h
        s = _dot_nt(jnp.where(sel, mq, 0.0).astype(BF16), mk)
        p = jnp.exp(s - jnp.max(s, axis=-1, keepdims=True))
        o = _dot(p.astype(BF16), mv) / jnp.sum(p, axis=-1, keepdims=True)
        out = jnp.where(sel, o, out)
    return out


def _lambda(lf):
    s1 = jnp.sum(lf[0:1] * lf[1:2], axis=-1, keepdims=True)
    s2 = jnp.sum(lf[2:3] * lf[3:4], axis=-1, keepdims=True)
    return jnp.exp(s1) - jnp.exp(s2) + LAM_INIT_B


def _mem_kv_kernel(mem_ref, g_ref, w_ref, kg_ref, k_ref, v_ref):
    h = _rms(mem_ref[0], g_ref[0]).astype(BF16)
    kv = _dot(h, w_ref[0])
    k_ref[0, 0] = _group_rms(kv[:, :MEM_W], kg_ref[0], MEM_HD)
    v_ref[0, 0] = kv[:, MEM_W:]


def _mem_kv(mem, norm_mem, w_mem_kv, mem_k_gain):
    depth, batch = w_mem_kv.shape[0], mem.shape[0]
    out = jax.ShapeDtypeStruct((depth, batch, MEM_LEN, MEM_W), F32)
    return pl.pallas_call(
        _mem_kv_kernel,
        grid=(depth, batch),
        in_specs=[
            pl.BlockSpec((1, MEM_LEN, D_MODEL), lambda l, b: (b, 0, 0)),
            pl.BlockSpec((1, 1, D_MODEL), lambda l, b: (l, 0, 0)),
            pl.BlockSpec((1, D_MODEL, 2 * MEM_W), lambda l, b: (l, 0, 0)),
            pl.BlockSpec((1, 1, MEM_W), lambda l, b: (l, 0, 0)),
        ],
        out_specs=[pl.BlockSpec((1, 1, MEM_LEN, MEM_W), lambda l, b: (l, b, 0, 0))] * 2,
        out_shape=[out, out],
        compiler_params=_params(2),
        name="mem_kv",
    )(mem, norm_mem, w_mem_kv, mem_k_gain)


def _conv31_tile(ext_ref, cw_ref, cb_ref, r0, lanes):
    span = CONV_ROWS + CARRY_ROWS + 8
    e = ext_ref[r0:r0 + span, lanes]
    acc = jnp.broadcast_to(cb_ref[:, lanes], (CONV_ROWS, CONV_LANES))
    for r in range(8):
        u = e if r == 0 else pltpu.roll(e, span - r, 0)
        for o in range(2, CONV_A_WIDTH + 2):
            if o % 8 == r:
                a = o // 8
                acc = acc + cw_ref[o - 2:o - 1, lanes] * u[8 * a:8 * a + CONV_ROWS]
    return acc


def _p_mixa_kernel(x_ref, gmix_ref, win_ref, cw_ref, cb_ref, lng_ref, lnb_ref, mqg_ref, mk_ref, mv_ref,
                   wout_ref, x1_ref, st_ref, ext_ref, cm_ref):
    tile = x_ref.shape[1]

    @pl.when(pl.program_id(1) == 0)
    def _():
        ext_ref[0:CARRY_ROWS, :] = jnp.zeros((CARRY_ROWS, C_A), F32)
        ext_ref[CARRY_ROWS + tile:CARRY_ROWS + tile + 8, :] = jnp.zeros((8, C_A), F32)

    x = x_ref[0]
    h = _rms(x, gmix_ref[...]).astype(BF16)
    z = _dot(h, win_ref[...])
    ext_ref[CARRY_ROWS:CARRY_ROWS + tile, :] = z[:, :C_A] * jax.nn.sigmoid(z[:, C_A:2 * C_A])

    for r0 in range(0, tile, CONV_ROWS):
        y = jnp.concatenate(
            [_conv31_tile(ext_ref, cw_ref, cb_ref, r0, slice(l0, l0 + CONV_LANES))
             for l0 in range(0, C_A, CONV_LANES)], axis=-1)
        c = _silu(_layernorm(y, lng_ref[...], lnb_ref[...]))
        cm_ref[r0:r0 + CONV_ROWS, 0:C_A] = c.astype(BF16)

    st_ref[0] = ext_ref[tile:tile + CARRY_ROWS, :]
    ext_ref[0:CARRY_ROWS, :] = ext_ref[tile:tile + CARRY_ROWS, :]

    mq = _group_rms(z[:, 2 * C_A:], mqg_ref[...], MEM_HD) * (MEM_HD ** -0.5)
    m = _mem_attend_shared(mq, mk_ref[0, 0].astype(BF16), mv_ref[0, 0].astype(BF16))
    cm_ref[:, C_A:] = m.astype(BF16)
    x1_ref[0] = x + _dot(cm_ref[...], wout_ref[...])


def _p_mixa(x, gmix, win, cw, cb, lng, lnb, mqg, mem_k, mem_v, wout):
    batch, seq, _ = x.shape
    tile = ROW_TILE
    tok = pl.BlockSpec((1, tile, D_MODEL), lambda b, t: (b, t, 0))
    mem = pl.BlockSpec((1, 1, MEM_LEN, MEM_W), lambda b, t: (0, b, 0, 0))
    return pl.pallas_call(
        _p_mixa_kernel,
        grid=(batch, seq // tile),
        in_specs=[tok, _const_spec(gmix.shape), _const_spec(win.shape), _const_spec(cw.shape),
                  _const_spec(cb.shape), _const_spec(lng.shape), _const_spec(lnb.shape),
                  _const_spec(mqg.shape), mem, mem, _const_spec(wout.shape)],
        out_specs=[tok, pl.BlockSpec((1, CARRY_ROWS, C_A), lambda b, t: (b, 0, 0))],
        out_shape=[jax.ShapeDtypeStruct(x.shape, F32),
                   jax.ShapeDtypeStruct((batch, CARRY_ROWS, C_A), F32)],
        scratch_shapes=[pltpu.VMEM((CARRY_ROWS + tile + 8, C_A), F32),
                        pltpu.VMEM((tile, D_MODEL), BF16)],
        compiler_params=_params(2),
        name="p_mixa",
    )(x, gmix, win, cw, cb, lng, lnb, mqg, mem_k, mem_v, wout)


def _ffn_hidden(h, wup_ref, cw_ref, cb_ref, conv_fn, act_ref):
    for j in range(N_FF_CHUNKS):
        lanes = slice(j * FF_CHUNK, (j + 1) * FF_CHUNK)
        gate = _dot(h, wup_ref[:, lanes])
        val = _dot(h, wup_ref[:, D_FF + j * FF_CHUNK:D_FF + (j + 1) * FF_CHUNK])
        g2, g1 = conv_fn(lanes, gate)
        conv = (cw_ref[0:1, lanes] * g2 + cw_ref[1:2, lanes] * g1 + cw_ref[2:3, lanes] * gate
                + cb_ref[:, lanes])
        act_ref[:, lanes] = (jax.nn.gelu(conv) * val).astype(BF16)


def _p_ffn_kernel(x_ref, g_ref, wup_ref, cw_ref, cb_ref, wdn_ref, o_ref, st_ref, carry_ref, act_ref):
    tile = x_ref.shape[1]

    @pl.when(pl.program_id(1) == 0)
    def _():
        carry_ref[...] = jnp.zeros(carry_ref.shape, F32)

    x = x_ref[0]
    h = _rms(x, g_ref[...]).astype(BF16)
    row = lax.broadcasted_iota(jnp.int32, (tile, FF_CHUNK), 0)

    def shifted(lanes, gate):
        c2 = carry_ref[0:1, lanes]
        c1 = carry_ref[1:2, lanes]
        g1 = jnp.where(row == 0, c1, pltpu.roll(gate, 1, 0))
        g2 = jnp.where(row == 0, c2, jnp.where(row == 1, c1, pltpu.roll(gate, 2, 0)))
        carry_ref[0:2, lanes] = gate[tile - 2:tile]
        return g2, g1

    _ffn_hidden(h, wup_ref, cw_ref, cb_ref, shifted, act_ref)
    o_ref[0] = x + _dot(act_ref[...], wdn_ref[...])
    st_ref[0] = carry_ref[0:2, :]


def _p_ffn(x, g, wup, cw, cb, wdn):
    batch, seq, _ = x.shape
    tile = ROW_TILE
    tok = pl.BlockSpec((1, tile, D_MODEL), lambda b, t: (b, t, 0))
    return pl.pallas_call(
        _p_ffn_kernel,
        grid=(batch, seq // tile),
        in_specs=[tok, _const_spec(g.shape), _const_spec(wup.shape), _const_spec(cw.shape),
                  _const_spec(cb.shape), _const_spec(wdn.shape)],
        out_specs=[tok, pl.BlockSpec((1, 2, D_FF), lambda b, t: (b, 0, 0))],
        out_shape=[jax.ShapeDtypeStruct(x.shape, F32), jax.ShapeDtypeStruct((batch, 2, D_FF), F32)],
        scratch_shapes=[pltpu.VMEM((8, D_FF), F32), pltpu.VMEM((tile, D_FF), BF16)],
        compiler_params=_params(2),
        name="p_ffn",
    )(x, g, wup, cw, cb, wdn)


def _s_ffn_kernel(x_ref, c_ref, m_ref, wout_ref, g_ref, wup_ref, cw_ref, cb_ref, wdn_ref, st_ref,
                  o_ref, nst_ref, act_ref):
    n_seq = st_ref.shape[1]
    o_ref[...] = (x_ref[...] + _dot(c_ref[...], wout_ref[0:DIFF_W, :])
                  + _dot(m_ref[...], wout_ref[DIFF_W:, :]))
    h = _rms(o_ref[...], g_ref[...]).astype(BF16)

    def shifted(lanes, gate):
        ext = [st_ref[0, :, lanes], st_ref[1, :, lanes]] + [
            gate[t * n_seq:(t + 1) * n_seq] for t in range(gate.shape[0] // n_seq)]
        nst_ref[0, :, lanes] = ext[-2]
        nst_ref[1, :, lanes] = ext[-1]
        return jnp.concatenate(ext[:-2], axis=0), jnp.concatenate(ext[1:-1], axis=0)

    _ffn_hidden(h, wup_ref, cw_ref, cb_ref, shifted, act_ref)
    o_ref[...] = o_ref[...] + _dot(act_ref[...], wdn_ref[...])


def _s_ffn(x, c, m, wout, g, wup, cw, cb, wdn, st):
    args = (x, c, m, wout, g, wup, cw, cb, wdn, st)
    return pl.pallas_call(
        _s_ffn_kernel,
        grid=(1,),
        in_specs=[_const_spec(a.shape) for a in args],
        out_specs=[_const_spec(x.shape), _const_spec(st.shape)],
        out_shape=[jax.ShapeDtypeStruct(x.shape, F32), jax.ShapeDtypeStruct(st.shape, F32)],
        scratch_shapes=[pltpu.VMEM((x.shape[0], D_FF), BF16)],
        compiler_params=_params(1),
        name="s_ffn",
    )(*args)


def _mixb_project(x, gkv, wkv, kg, gmix, winb, qg, mqg):
    xn = x * lax.rsqrt(jnp.mean(x * x, axis=-1, keepdims=True) + EPS)
    kv = _dot((xn * gkv).astype(BF16), wkv)
    qm = _dot((xn * gmix).astype(BF16), winb)
    k = _group_rms(kv[:, :DIFF_W], kg, DIFF_HD)
    v = kv[:, DIFF_W:]
    q = _group_rms(qm[:, :DIFF_W], qg, DIFF_HD) * (DIFF_HD ** -0.5)
    mq = _group_rms(qm[:, DIFF_W:], mqg, MEM_HD) * (MEM_HD ** -0.5)
    return k, v, q, mq


def _p_mixb_kernel(x_ref, gkv_ref, wkv_ref, kg_ref, gmix_ref, winb_ref, qg_ref, mqg_ref, mk_ref, mv_ref,
                   k_ref, v_ref, kb_ref, vb_ref, qb_ref, mb_ref):
    k, v, q, mq = _mixb_project(x_ref[0], gkv_ref[...], wkv_ref[...], kg_ref[...], gmix_ref[...],
                                winb_ref[...], qg_ref[...], mqg_ref[...])
    k_ref[0] = k
    v_ref[0] = v
    kb_ref[0] = k.astype(BF16)
    vb_ref[0] = v.astype(BF16)
    qb_ref[0] = q.astype(BF16)
    m = _mem_attend_shared(mq, mk_ref[0, 0].astype(BF16), mv_ref[0, 0].astype(BF16))
    mb_ref[0] = m.astype(BF16)


def _p_mixb(x, gkv, wkv, kg, gmix, winb, qg, mqg, mem_k, mem_v):
    batch, seq, _ = x.shape
    tile = ROW_TILE
    tok = lambda w: pl.BlockSpec((1, tile, w), lambda b, t: (b, t, 0))
    mem = pl.BlockSpec((1, 1, MEM_LEN, MEM_W), lambda b, t: (1, b, 0, 0))
    shp = lambda w, dt: jax.ShapeDtypeStruct((batch, seq, w), dt)
    return pl.pallas_call(
        _p_mixb_kernel,
        grid=(batch, seq // tile),
        in_specs=[tok(D_MODEL)] + [_const_spec(a.shape) for a in (gkv, wkv, kg, gmix, winb, qg, mqg)]
        + [mem, mem],
        out_specs=[tok(DIFF_W)] * 5 + [tok(MEM_W)],
        out_shape=[shp(DIFF_W, F32), shp(DIFF_W, F32), shp(DIFF_W, BF16), shp(DIFF_W, BF16),
                   shp(DIFF_W, BF16), shp(MEM_W, BF16)],
        compiler_params=_params(2),
        name="p_mixb",
    )(x, gkv, wkv, kg, gmix, winb, qg, mqg, mem_k, mem_v)


def _s_mixb_kernel(x_ref, gkv_ref, wkv_ref, kg_ref, gmix_ref, winb_ref, qg_ref, mqg_ref,
                   k_ref, v_ref, q_ref, mq_ref):
    k, v, q, mq = _mixb_project(x_ref[...], gkv_ref[...], wkv_ref[...], kg_ref[...], gmix_ref[...],
                                winb_ref[...], qg_ref[...], mqg_ref[...])
    k_ref[...] = k
    v_ref[...] = v
    q_ref[...] = q
    mq_ref[...] = mq


def _s_mixb(x, gkv, wkv, kg, gmix, winb, qg, mqg):
    n = x.shape[0]
    args = (x, gkv, wkv, kg, gmix, winb, qg, mqg)
    widths = (DIFF_W, DIFF_W, DIFF_W, MEM_W)
    return pl.pallas_call(
        _s_mixb_kernel,
        grid=(1,),
        in_specs=[_const_spec(a.shape) for a in args],
        out_specs=[_const_spec((n, w)) for w in widths],
        out_shape=[jax.ShapeDtypeStruct((n, w), F32) for w in widths],
        compiler_params=_params(1),
        name="s_mixb",
    )(*args)


def _p_att_kernel(q_ref, k_ref, v_ref, m_ref, x_ref, lam_ref, sg_ref, wout_ref, o_ref, c_ref):
    tq = q_ref.shape[1]
    qi = pl.program_id(1)
    lam = _lambda(lam_ref[...])
    row = lax.broadcasted_iota(jnp.int32, (tq, tq), 0)
    col = lax.broadcasted_iota(jnp.int32, (tq, tq), 1)
    causal = col <= row
    first_map = lax.broadcasted_iota(jnp.int32, (tq, DIFF_VD), 1) < DIFF_HD

    for h in range(DIFF_HEADS):
        lanes = slice(h * DIFF_VD, (h + 1) * DIFF_VD)
        q = q_ref[0, :, lanes]
        zero = jnp.zeros_like(q)
        qs = (jnp.where(first_map, q, zero), jnp.where(first_map, zero, q))

        def block(j, carry, masked):
            start = pl.multiple_of(j * tq, tq)
            kb = k_ref[0, pl.ds(start, tq), lanes]
            vb = v_ref[0, pl.ds(start, tq), lanes]
            new = []
            for c in range(2):
                m_old, l_old, acc = carry[c]
                s = _dot_nt(qs[c], kb)
                if masked:
                    s = jnp.where(causal, s, NEG_INF)
                m_new = jnp.maximum(m_old, jnp.max(s, axis=-1, keepdims=True))
                alpha = jnp.exp(m_old - m_new)
                p = jnp.exp(s - m_new)
                new.append((m_new, alpha * l_old + jnp.sum(p, axis=-1, keepdims=True),
                            alpha * acc + _dot(p.astype(BF16), vb)))
            return tuple(new)

        init = tuple((jnp.full((tq, 1), NEG_INF, F32), jnp.zeros((tq, 1), F32),
                      jnp.zeros((tq, DIFF_VD), F32)) for _ in range(2))
        carry = lax.fori_loop(0, qi, lambda j, c: block(j, c, False), init)
        (_, l1, a1), (_, l2, a2) = block(qi, carry, True)
        o = a1 / l1 - lam * (a2 / l2)
        o = _rms(o, sg_ref[:, lanes]) * (1.0 - LAM_INIT_B)
        c_ref[:, lanes] = o.astype(BF16)

    o_ref[0] = (x_ref[0] + _dot(c_ref[...], wout_ref[0:DIFF_W, :])
                + _dot(m_ref[0], wout_ref[DIFF_W:, :]))


def _p_att(q, k, v, m, x, lam, sg, wout):
    batch, seq, _ = x.shape
    tile = ROW_TILE
    tok = lambda w: pl.BlockSpec((1, tile, w), lambda b, t: (b, t, 0))
    kv = pl.BlockSpec((1, seq, DIFF_W), lambda b, t: (b, 0, 0), pipeline_mode=pl.Buffered(1))
    return pl.pallas_call(
        _p_att_kernel,
        grid=(batch, seq // tile),
        in_specs=[tok(DIFF_W), kv, kv, tok(MEM_W), tok(D_MODEL), _const_spec(lam.shape),
                  _const_spec(sg.shape), _const_spec(wout.shape)],
        out_specs=tok(D_MODEL),
        out_shape=jax.ShapeDtypeStruct(x.shape, F32),
        scratch_shapes=[pltpu.VMEM((tile, DIFF_W), BF16)],
        compiler_params=_params(2),
        name="p_att",
    )(q, k, v, m, x, lam, sg, wout)


def _s_mixa_kernel(x_ref, st_ref, gmix_ref, win_ref, cw_ref, cb_ref, lng_ref, lnb_ref, mqg_ref,
                   glu_ref, c_ref, mq_ref):
    n_t, nb, _ = x_ref.shape
    n_hist = st_ref.shape[0]
    x = x_ref[...].reshape(n_t * nb, D_MODEL)
    h = _rms(x, gmix_ref[...]).astype(BF16)
    z = _dot(h, win_ref[...])
    glu = z[:, :C_A] * jax.nn.sigmoid(z[:, C_A:2 * C_A])
    glu_ref[...] = glu.reshape(n_t, nb, C_A)
    mq = _group_rms(z[:, 2 * C_A:], mqg_ref[...], MEM_HD) * (MEM_HD ** -0.5)
    mq_ref[...] = mq.reshape(n_t, nb, MEM_W)

    ys = [[] for _ in range(n_t)]
    for l0 in range(0, C_A, CONV_LANES):
        lanes = slice(l0, l0 + CONV_LANES)
        acc = [jnp.broadcast_to(cb_ref[:, lanes], (nb, CONV_LANES)) for _ in range(n_t)]
        for r in range(n_hist + n_t):
            e = st_ref[r, :, lanes] if r < n_hist else glu[(r - n_hist) * nb:(r - n_hist + 1) * nb, lanes]
            for t in range(n_t):
                if 0 <= r - t < CONV_A_WIDTH:
                    acc[t] = acc[t] + cw_ref[r - t:r - t + 1, lanes] * e
        for t in range(n_t):
            ys[t].append(acc[t])
    for t in range(n_t):
        y = jnp.concatenate(ys[t], axis=-1)
        c_ref[t] = _silu(_layernorm(y, lng_ref[...], lnb_ref[...])).astype(BF16)


def _s_mixa(x_tm, st_tm, gmix, win, cw, cb, lng, lnb, mqg):
    n_t, n_seq, _ = x_tm.shape
    nb = 32
    blk = lambda r, w: pl.BlockSpec((r, nb, w), lambda i: (0, i, 0))
    consts = (gmix, win, cw, cb, lng, lnb, mqg)
    return pl.pallas_call(
        _s_mixa_kernel,
        grid=(n_seq // nb,),
        in_specs=[blk(n_t, D_MODEL), blk(st_tm.shape[0], C_A)] + [_const_spec(a.shape) for a in consts],
        out_specs=[blk(n_t, C_A), blk(n_t, C_A), blk(n_t, MEM_W)],
        out_shape=[jax.ShapeDtypeStruct((n_t, n_seq, C_A), F32),
                   jax.ShapeDtypeStruct((n_t, n_seq, C_A), BF16),
                   jax.ShapeDtypeStruct((n_t, n_seq, MEM_W), F32)],
        compiler_params=_params(1),
        name="s_mixa",
    )(x_tm, st_tm, *consts)


def _s_mem_kernel(q_ref, mk_ref, mv_ref, o_ref):
    nb = q_ref.shape[0]
    rows = MEM_HEADS * PAD_T
    head_of_row = jnp.right_shift(lax.broadcasted_iota(jnp.int32, (rows, MEM_W), 0), 3)
    own = head_of_row == _lane_group((rows, MEM_W), 6)
    head_of_lane = _lane_group((PAD_T, MEM_W), 6)
    for b in range(nb):
        q = q_ref[b]
        qb = jnp.where(own, jnp.concatenate([q] * MEM_HEADS, axis=0), 0.0).astype(BF16)
        s = _dot_nt(qb, mk_ref[0, b].astype(BF16))
        p = jnp.exp(s - jnp.max(s, axis=-1, keepdims=True))
        o = _dot(p.astype(BF16), mv_ref[0, b].astype(BF16)) / jnp.sum(p, axis=-1, keepdims=True)
        out = jnp.zeros((PAD_T, MEM_W), F32)
        for h in range(MEM_HEADS):
            out = jnp.where(head_of_lane == h, o[h * PAD_T:(h + 1) * PAD_T], out)
        o_ref[b] = out


def _s_mem(q_pad, mem_k, mem_v, layer):
    n_seq = q_pad.shape[0]
    nb = 8
    tok = pl.BlockSpec((nb, PAD_T, MEM_W), lambda i: (i, 0, 0))
    mem = pl.BlockSpec((1, nb, MEM_LEN, MEM_W), lambda i: (layer, i, 0, 0))
    return pl.pallas_call(
        _s_mem_kernel,
        grid=(n_seq // nb,),
        in_specs=[tok, mem, mem],
        out_specs=tok,
        out_shape=jax.ShapeDtypeStruct(q_pad.shape, F32),
        compiler_params=_params(1),
        name="s_mem",
    )(q_pad, mem_k, mem_v)


def _page_copies(pt_ref, ck_hbm, cv_hbm, kbuf, vbuf, sem, seq, slot, n_pages):
    copies = []
    for p in range(n_pages):
        page = pt_ref[seq * n_pages + p]
        for h in range(DIFF_HEADS):
            copies.append(pltpu.make_async_copy(ck_hbm.at[page, :, h, :], kbuf.at[slot, p, h], sem.at[0, slot]))
            copies.append(pltpu.make_async_copy(cv_hbm.at[page, :, h, :], vbuf.at[slot, p, h], sem.at[1, slot]))
    return copies


def _s_paged_kernel(pt_ref, q_ref, kn_ref, vn_ref, lam_ref, sg_ref, ck_hbm, cv_hbm, o_ref,
                    kbuf, vbuf, sem, *, n_pages, n_new):
    seq = pl.program_id(0)
    slot = seq % 2
    fetch = functools.partial(_page_copies, pt_ref, ck_hbm, cv_hbm, kbuf, vbuf, sem, n_pages=n_pages)

    @pl.when(seq == 0)
    def _():
        for c in fetch(seq=0, slot=0):
            c.start()

    @pl.when(seq + 1 < pl.num_programs(0))
    def _():
        for c in fetch(seq=seq + 1, slot=1 - slot):
            c.start()

    for c in fetch(seq=seq, slot=slot):
        c.wait()
    k_refs = [kbuf.at[slot, p] for p in range(n_pages)]
    v_refs = [vbuf.at[slot, p] for p in range(n_pages)]
    lam = _lambda(lam_ref[...])
    first_map = lax.broadcasted_iota(jnp.int32, (PAD_T, DIFF_VD), 1) < DIFF_HD
    t_q = jnp.bitwise_and(lax.broadcasted_iota(jnp.int32, (2 * PAD_T, 1), 0), PAD_T - 1)

    outs = []
    for h in range(DIFF_HEADS):
        lanes = slice(h * DIFF_VD, (h + 1) * DIFF_VD)
        q = q_ref[0, :, lanes]
        qf = jnp.concatenate([jnp.where(first_map, q, 0.0), jnp.where(first_map, 0.0, q)], axis=0)
        qb = qf.astype(BF16)

        s_past = [_dot_nt(qb, k_refs[p][h].astype(BF16)) for p in range(n_pages)]
        s_new = [jnp.where(t <= t_q,
                           jnp.sum(qf * kn_ref[0, t:t + 1, lanes], axis=-1, keepdims=True), NEG_INF)
                 for t in range(n_new)]

        m_el = s_past[0]
        for s in s_past[1:]:
            m_el = jnp.maximum(m_el, s)
        m = jnp.max(m_el, axis=-1, keepdims=True)
        for s in s_new:
            m = jnp.maximum(m, s)

        p_past = [jnp.exp(s - m) for s in s_past]
        p_new = [jnp.exp(s - m) for s in s_new]
        l_el = p_past[0]
        for p in p_past[1:]:
            l_el = l_el + p
        l = jnp.sum(l_el, axis=-1, keepdims=True)
        for p in p_new:
            l = l + p
        w1 = 1.0 / l[:PAD_T]
        w2 = lam / l[PAD_T:]

        acc = jnp.zeros((PAD_T, DIFF_VD), F32)
        for i in range(n_pages):
            a = (p_past[i][:PAD_T] * w1 - p_past[i][PAD_T:] * w2).astype(BF16)
            acc = acc + _dot(a, v_refs[i][h].astype(BF16))
        for t in range(n_new):
            acc = acc + (p_new[t][:PAD_T] * w1 - p_new[t][PAD_T:] * w2) * vn_ref[0, t:t + 1, lanes]
        outs.append(_rms(acc, sg_ref[:, lanes]) * (1.0 - LAM_INIT_B))
    o_ref[0] = jnp.concatenate(outs, axis=-1)


def _s_paged(page_table, q_pad, k_new, v_new, lam, sg, cache_k, cache_v):
    n_seq, n_pages = page_table.shape
    n_new = k_new.shape[1]
    tok = lambda r: pl.BlockSpec((1, r, DIFF_W), lambda b, pt: (b, 0, 0))
    hbm = pl.BlockSpec(memory_space=pl.ANY)
    page_buf = pltpu.VMEM((2, n_pages, DIFF_HEADS, PAGE_SIZE, DIFF_VD), F32)
    grid_spec = pltpu.PrefetchScalarGridSpec(
        num_scalar_prefetch=1,
        grid=(n_seq,),
        in_specs=[tok(PAD_T), tok(n_new), tok(n_new), _const_spec(lam.shape), _const_spec(sg.shape),
                  hbm, hbm],
        out_specs=tok(PAD_T),
        scratch_shapes=[page_buf, page_buf, pltpu.SemaphoreType.DMA((2, 2))],
    )
    return pl.pallas_call(
        functools.partial(_s_paged_kernel, n_pages=n_pages, n_new=n_new),
        grid_spec=grid_spec,
        out_shape=jax.ShapeDtypeStruct(q_pad.shape, F32),
        compiler_params=_params(1),
        name="s_paged",
    )(page_table.reshape(-1), q_pad, k_new, v_new, lam, sg, cache_k, cache_v)


def _to_time_major(a):
    return jnp.swapaxes(a, 0, 1)


def _pad_tokens(a_tm):
    a = jnp.swapaxes(a_tm, 0, 1)
    return jnp.pad(a, ((0, 0), (0, PAD_T - a.shape[1]), (0, 0)))


def _from_padded(a, n_t, dtype):
    return jnp.swapaxes(a[:, :n_t], 0, 1).astype(dtype)


def kernel(x_prompt, x_sample, mem_prompt, cache_mem_k, cache_mem_v, state_conv_a, state_ffn_conv, cache_k, cache_v, page_table, norm_mix, norm_ffn, norm_mem, w_mem_kv, mem_q_norm, mem_k_norm, w_in_a, conv_a_w, conv_a_b, ln_a_g, ln_a_b, w_out_a, norm_kv, w_kv, k_norm, w_in_b, q_norm, lambdas, subln_g, w_out_b, w_up, ffn_conv_w, ffn_conv_b, w_down):
    batch, seq, _ = x_prompt.shape
    n_seq, n_t, _ = x_sample.shape
    depth = norm_mix.shape[0]
    assert depth == 2 and w_in_a.shape[0] == 1 and w_in_b.shape[0] == 1
    assert seq % ROW_TILE == 0 and n_t <= PAD_T and n_seq % 32 == 0

    row = lambda a: a.reshape(1, -1)
    gmix = [row(norm_mix[l]) for l in range(depth)]
    gffn = [row(norm_ffn[l]) for l in range(depth)]
    mqg = [row(jnp.tile(mem_q_norm[l], MEM_HEADS)) for l in range(depth)]
    mkg = jnp.tile(mem_k_norm, (1, MEM_HEADS)).reshape(depth, 1, MEM_W)
    kg = row(jnp.tile(k_norm.reshape(-1), DIFF_HEADS))
    qg = row(jnp.tile(q_norm[0].reshape(-1), DIFF_HEADS))
    sg = row(jnp.tile(subln_g[0], DIFF_HEADS))
    lam = lambdas[0]
    win_a = w_in_a[0].astype(BF16)
    wout_a = w_out_a[0].astype(BF16)
    wkv = w_kv.astype(BF16)
    win_b = w_in_b[0].astype(BF16)
    wout_b = w_out_b[0].astype(BF16)
    wup = [w_up[l].astype(BF16) for l in range(depth)]
    wdn = [w_down[l].astype(BF16) for l in range(depth)]
    cw_a, cb_a = conv_a_w[0], row(conv_a_b[0])
    lng, lnb = row(ln_a_g[0]), row(ln_a_b[0])
    fcw = [ffn_conv_w[l] for l in range(depth)]
    fcb = [row(ffn_conv_b[l]) for l in range(depth)]

    p_mem_k, p_mem_v = _mem_kv(mem_prompt, norm_mem.reshape(depth, 1, D_MODEL), w_mem_kv.astype(BF16), mkg)
    x1, conv_tail = _p_mixa(x_prompt, gmix[0], win_a, cw_a, cb_a, lng, lnb, mqg[0], p_mem_k, p_mem_v, wout_a)
    x2, p_ffn0 = _p_ffn(x1, gffn[0], wup[0], fcw[0], fcb[0], wdn[0])
    p_k, p_v, kb, vb, qb, mb = _p_mixb(x2, row(norm_kv), wkv, kg, gmix[1], win_b, qg, mqg[1], p_mem_k, p_mem_v)
    x3 = _p_att(qb, kb, vb, mb, x2, lam, sg, wout_b)
    y_prompt, p_ffn1 = _p_ffn(x3, gffn[1], wup[1], fcw[1], fcb[1], wdn[1])

    hd = (MEM_HEADS, MEM_HD)
    p_mem_k = p_mem_k.reshape(depth, batch, MEM_LEN, *hd)
    p_mem_v = p_mem_v.reshape(depth, batch, MEM_LEN, *hd)
    p_conv_a = conv_tail[None, :, CARRY_ROWS - (CONV_A_WIDTH - 1):, :]
    p_ffn_conv = jnp.stack([p_ffn0, p_ffn1])
    p_k = p_k.reshape(batch, seq, DIFF_HEADS, DIFF_VD)
    p_v = p_v.reshape(batch, seq, DIFF_HEADS, DIFF_VD)

    smem_k = cache_mem_k.reshape(depth, n_seq, MEM_LEN, MEM_W)
    smem_v = cache_mem_v.reshape(depth, n_seq, MEM_LEN, MEM_W)
    xs = _to_time_major(x_sample)
    glu, c_a, mq_a = _s_mixa(xs, _to_time_major(state_conv_a[0]), gmix[0], win_a, cw_a, cb_a, lng, lnb, mqg[0])
    m_a = _from_padded(_s_mem(_pad_tokens(mq_a), smem_k, smem_v, 0), n_t, BF16)
    flat = lambda a: a.reshape(n_t * n_seq, a.shape[-1])
    xs2, s_ffn0 = _s_ffn(flat(xs), flat(c_a), flat(m_a), wout_a, gffn[0], wup[0], fcw[0], fcb[0], wdn[0],
                         _to_time_major(state_ffn_conv[0]))
    k_new, v_new, q_s, mq_b = _s_mixb(xs2, row(norm_kv), wkv, kg, gmix[1], win_b, qg, mqg[1])
    unflat = lambda a: a.reshape(n_t, n_seq, a.shape[-1])
    m_b = _from_padded(_s_mem(_pad_tokens(unflat(mq_b)), smem_k, smem_v, 1), n_t, BF16)
    k_new_sm = jnp.swapaxes(unflat(k_new), 0, 1)
    v_new_sm = jnp.swapaxes(unflat(v_new), 0, 1)
    c_b = _s_paged(page_table, _pad_tokens(unflat(q_s)), k_new_sm, v_new_sm, lam, sg, cache_k, cache_v)
    c_b = _from_padded(c_b, n_t, BF16)
    ys, s_ffn1 = _s_ffn(xs2, flat(c_b), flat(m_b), wout_b, gffn[1], wup[1], fcw[1], fcb[1], wdn[1],
                        _to_time_major(state_ffn_conv[1]))

    y_sample = jnp.swapaxes(unflat(ys), 0, 1)
    s_conv_a = jnp.concatenate([state_conv_a[0][:, n_t:], jnp.swapaxes(glu, 0, 1)], axis=1)[None]
    s_ffn_conv = jnp.stack([_to_time_major(s_ffn0), _to_time_major(s_ffn1)])
    s_k = k_new_sm.reshape(n_seq, n_t, DIFF_HEADS, DIFF_VD)
    s_v = v_new_sm.reshape(n_seq, n_t, DIFF_HEADS, DIFF_VD)
    return (y_prompt, y_sample, p_mem_k, p_mem_v, p_conv_a, p_ffn_conv, p_k, p_v,
            s_conv_a, s_ffn_conv, s_k, s_v)
```

```python
import functools
import math

import jax
import jax.numpy as jnp
from jax import lax
from jax.experimental import pallas as pl
from jax.experimental.pallas import tpu as pltpu

F32 = jnp.float32
BF16 = jnp.bfloat16

D_MODEL = 1024
MEM_LEN = 256
MEM_HEADS = 4
MEM_W = 256
MEM_HD = 64
C_A = 768
CONV_A_WIDTH = 31
DIFF_HD = 64
DIFF_VD = 128
DIFF_HEADS = 6
DIFF_W = 768
D_FF = 2816
PAGE_SIZE = 128
EPS = 1e-6
LN_EPS = 1e-5
NEG_INF = -1e30
LAM_INIT_B = 0.8 - 0.6 * math.exp(-0.3 * 1)

FF_CHUNK = 256
N_FF_CHUNKS = D_FF // FF_CHUNK
ROW_TILE = 512
CONV_ROWS = 128
CONV_LANES = 256
CARRY_ROWS = 32
PAD_T = 8
VMEM_LIMIT = 56 * 1024 * 1024


def _const_spec(shape):
    n = len(shape)
    return pl.BlockSpec(shape, lambda *_: (0,) * n, pipeline_mode=pl.Buffered(1))


def _params(n_grid):
    return pltpu.CompilerParams(dimension_semantics=("arbitrary",) * n_grid,
                                vmem_limit_bytes=VMEM_LIMIT)


def _dot(a, b):
    return jnp.dot(a, b, preferred_element_type=F32)


def _dot_nt(a, b):
    return lax.dot_general(a, b, (((1,), (1,)), ((), ())), preferred_element_type=F32)


def _rms(x, g):
    return x * lax.rsqrt(jnp.mean(x * x, axis=-1, keepdims=True) + EPS) * g


def _group_rms(x, g, group):
    width = x.shape[-1]
    blk = 256
    r = lax.broadcasted_iota(jnp.int32, (blk, blk), 0)
    c = lax.broadcasted_iota(jnp.int32, (blk, blk), 1)
    shift = group.bit_length() - 1
    ones = (jnp.right_shift(r, shift) == jnp.right_shift(c, shift)).astype(BF16)
    outs = []
    for i in range(width // blk):
        xc = x[:, i * blk:(i + 1) * blk]
        ss = _dot((xc * xc).astype(BF16), ones)
        outs.append(xc * lax.rsqrt(ss * (1.0 / group) + EPS) * g[:, i * blk:(i + 1) * blk])
    return outs[0] if len(outs) == 1 else jnp.concatenate(outs, axis=-1)


def _layernorm(x, g, b):
    xc = x - jnp.mean(x, axis=-1, keepdims=True)
    var = jnp.mean(xc * xc, axis=-1, keepdims=True)
    return xc * lax.rsqrt(var + LN_EPS) * g + b


def _silu(x):
    return x * jax.nn.sigmoid(x)


def _lane_group(shape, shift):
    return jnp.right_shift(lax.broadcasted_iota(jnp.int32, shape, len(shape) - 1), shift)


def _mem_attend_shared(mq, mk, mv):
    head_of_lane = _lane_group(mq.shape, 6)
    out = jnp.zeros(mq.shape, F32)
    for h in range(MEM_HEADS):
        sel = head_of_lane == h
        s = _dot_nt(jnp.where(sel, mq, 0.0).astype(BF16), mk)
        p = jnp.exp(s - jnp.max(s, axis=-1, keepdims=True))
        o = _dot(p.astype(BF16), mv) / jnp.sum(p, axis=-1, keepdims=True)
        out = jnp.where(sel, o, out)
    return out


def _lambda(lf):
    s1 = jnp.sum(lf[0:1] * lf[1:2], axis=-1, keepdims=True)
    s2 = jnp.sum(lf[2:3] * lf[3:4], axis=-1, keepdims=True)
    return jnp.exp(s1) - jnp.exp(s2) + LAM_INIT_B


def _mem_kv_kernel(mem_ref, g_ref, w_ref, kg_ref, k_ref, v_ref):
    h = _rms(mem_ref[0], g_ref[0]).astype(BF16)
    kv = _dot(h, w_ref[0])
    k_ref[0, 0] = _group_rms(kv[:, :MEM_W], kg_ref[0], MEM_HD)
    v_ref[0, 0] = kv[:, MEM_W:]


def _mem_kv(mem, norm_mem, w_mem_kv, mem_k_gain):
    depth, batch = w_mem_kv.shape[0], mem.shape[0]
    out = jax.ShapeDtypeStruct((depth, batch, MEM_LEN, MEM_W), F32)
    return pl.pallas_call(
        _mem_kv_kernel,
        grid=(depth, batch),
        in_specs=[
            pl.BlockSpec((1, MEM_LEN, D_MODEL), lambda l, b: (b, 0, 0)),
            pl.BlockSpec((1, 1, D_MODEL), lambda l, b: (l, 0, 0)),
            pl.BlockSpec((1, D_MODEL, 2 * MEM_W), lambda l, b: (l, 0, 0)),
            pl.BlockSpec((1, 1, MEM_W), lambda l, b: (l, 0, 0)),
        ],
        out_specs=[pl.BlockSpec((1, 1, MEM_LEN, MEM_W), lambda l, b: (l, b, 0, 0))] * 2,
        out_shape=[out, out],
        compiler_params=_params(2),
        name="mem_kv",
    )(mem, norm_mem, w_mem_kv, mem_k_gain)


def _conv31_tile(ext_ref, cw_ref, cb_ref, r0, lanes):
    span = CONV_ROWS + CARRY_ROWS + 8
    e = ext_ref[r0:r0 + span, lanes]
    acc = jnp.broadcast_to(cb_ref[:, lanes], (CONV_ROWS, CONV_LANES))
    for r in range(8):
        u = e if r == 0 else pltpu.roll(e, span - r, 0)
        for o in range(2, CONV_A_WIDTH + 2):
            if o % 8 == r:
                a = o // 8
                acc = acc + cw_ref[o - 2:o - 1, lanes] * u[8 * a:8 * a + CONV_ROWS]
    return acc


def _p_mixa_kernel(x_ref, gmix_ref, win_ref, cw_ref, cb_ref, lng_ref, lnb_ref, mqg_ref, mk_ref, mv_ref,
                   wout_ref, x1_ref, st_ref, ext_ref, cm_ref):
    tile = x_ref.shape[1]

    @pl.when(pl.program_id(1) == 0)
    def _():
        ext_ref[0:CARRY_ROWS, :] = jnp.zeros((CARRY_ROWS, C_A), F32)
        ext_ref[CARRY_ROWS + tile:CARRY_ROWS + tile + 8, :] = jnp.zeros((8, C_A), F32)

    x = x_ref[0]
    h = _rms(x, gmix_ref[...]).astype(BF16)
    z = _dot(h, win_ref[...])
    ext_ref[CARRY_ROWS:CARRY_ROWS + tile, :] = z[:, :C_A] * jax.nn.sigmoid(z[:, C_A:2 * C_A])

    for r0 in range(0, tile, CONV_ROWS):
        y = jnp.concatenate(
            [_conv31_tile(ext_ref, cw_ref, cb_ref, r0, slice(l0, l0 + CONV_LANES))
             for l0 in range(0, C_A, CONV_LANES)], axis=-1)
        c = _silu(_layernorm(y, lng_ref[...], lnb_ref[...]))
        cm_ref[r0:r0 + CONV_ROWS, 0:C_A] = c.astype(BF16)

    st_ref[0] = ext_ref[tile:tile + CARRY_ROWS, :]
    ext_ref[0:CARRY_ROWS, :] = ext_ref[tile:tile + CARRY_ROWS, :]

    mq = _group_rms(z[:, 2 * C_A:], mqg_ref[...], MEM_HD) * (MEM_HD ** -0.5)
    m = _mem_attend_shared(mq, mk_ref[0, 0].astype(BF16), mv_ref[0, 0].astype(BF16))
    cm_ref[:, C_A:] = m.astype(BF16)
    x1_ref[0] = x + _dot(cm_ref[...], wout_ref[...])


def _p_mixa(x, gmix, win, cw, cb, lng, lnb, mqg, mem_k, mem_v, wout):
    batch, seq, _ = x.shape
    tile = ROW_TILE
    tok = pl.BlockSpec((1, tile, D_MODEL), lambda b, t: (b, t, 0))
    mem = pl.BlockSpec((1, 1, MEM_LEN, MEM_W), lambda b, t: (0, b, 0, 0))
    return pl.pallas_call(
        _p_mixa_kernel,
        grid=(batch, seq // tile),
        in_specs=[tok, _const_spec(gmix.shape), _const_spec(win.shape), _const_spec(cw.shape),
                  _const_spec(cb.shape), _const_spec(lng.shape), _const_spec(lnb.shape),
                  _const_spec(mqg.shape), mem, mem, _const_spec(wout.shape)],
        out_specs=[tok, pl.BlockSpec((1, CARRY_ROWS, C_A), lambda b, t: (b, 0, 0))],
        out_shape=[jax.ShapeDtypeStruct(x.shape, F32),
                   jax.ShapeDtypeStruct((batch, CARRY_ROWS, C_A), F32)],
        scratch_shapes=[pltpu.VMEM((CARRY_ROWS + tile + 8, C_A), F32),
                        pltpu.VMEM((tile, D_MODEL), BF16)],
        compiler_params=_params(2),
        name="p_mixa",
    )(x, gmix, win, cw, cb, lng, lnb, mqg, mem_k, mem_v, wout)


def _ffn_hidden(h, wup_ref, cw_ref, cb_ref, conv_fn, act_ref):
    for j in range(N_FF_CHUNKS):
        lanes = slice(j * FF_CHUNK, (j + 1) * FF_CHUNK)
        gate = _dot(h, wup_ref[:, lanes])
        val = _dot(h, wup_ref[:, D_FF + j * FF_CHUNK:D_FF + (j + 1) * FF_CHUNK])
        g2, g1 = conv_fn(lanes, gate)
        conv = (cw_ref[0:1, lanes] * g2 + cw_ref[1:2, lanes] * g1 + cw_ref[2:3, lanes] * gate
                + cb_ref[:, lanes])
        act_ref[:, lanes] = (jax.nn.gelu(conv) * val).astype(BF16)


def _p_ffn_kernel(x_ref, g_ref, wup_ref, cw_ref, cb_ref, wdn_ref, o_ref, st_ref, carry_ref, act_ref):
    tile = x_ref.shape[1]

    @pl.when(pl.program_id(1) == 0)
    def _():
        carry_ref[...] = jnp.zeros(carry_ref.shape, F32)

    x = x_ref[0]
    h = _rms(x, g_ref[...]).astype(BF16)
    row = lax.broadcasted_iota(jnp.int32, (tile, FF_CHUNK), 0)

    def shifted(lanes, gate):
        c2 = carry_ref[0:1, lanes]
        c1 = carry_ref[1:2, lanes]
        g1 = jnp.where(row == 0, c1, pltpu.roll(gate, 1, 0))
        g2 = jnp.where(row == 0, c2, jnp.where(row == 1, c1, pltpu.roll(gate, 2, 0)))
        carry_ref[0:2, lanes] = gate[tile - 2:tile]
        return g2, g1

    _ffn_hidden(h, wup_ref, cw_ref, cb_ref, shifted, act_ref)
    o_ref[0] = x + _dot(act_ref[...], wdn_ref[...])
    st_ref[0] = carry_ref[0:2, :]


def _p_ffn(x, g, wup, cw, cb, wdn):
    batch, seq, _ = x.shape
    tile = ROW_TILE
    tok = pl.BlockSpec((1, tile, D_MODEL), lambda b, t: (b, t, 0))
    return pl.pallas_call(
        _p_ffn_kernel,
        grid=(batch, seq // tile),
        in_specs=[tok, _const_spec(g.shape), _const_spec(wup.shape), _const_spec(cw.shape),
                  _const_spec(cb.shape), _const_spec(wdn.shape)],
        out_specs=[tok, pl.BlockSpec((1, 2, D_FF), lambda b, t: (b, 0, 0))],
        out_shape=[jax.ShapeDtypeStruct(x.shape, F32), jax.ShapeDtypeStruct((batch, 2, D_FF), F32)],
        scratch_shapes=[pltpu.VMEM((8, D_FF), F32), pltpu.VMEM((tile, D_FF), BF16)],
        compiler_params=_params(2),
        name="p_ffn",
    )(x, g, wup, cw, cb, wdn)


def _s_ffn_kernel(x_ref, c_ref, m_ref, wout_ref, g_ref, wup_ref, cw_ref, cb_ref, wdn_ref, st_ref,
                  o_ref, nst_ref, act_ref):
    n_seq = st_ref.shape[1]
    o_ref[...] = (x_ref[...] + _dot(c_ref[...], wout_ref[0:DIFF_W, :])
                  + _dot(m_ref[...], wout_ref[DIFF_W:, :]))
    h = _rms(o_ref[...], g_ref[...]).astype(BF16)

    def shifted(lanes, gate):
        ext = [st_ref[0, :, lanes], st_ref[1, :, lanes]] + [
            gate[t * n_seq:(t + 1) * n_seq] for t in range(gate.shape[0] // n_seq)]
        nst_ref[0, :, lanes] = ext[-2]
        nst_ref[1, :, lanes] = ext[-1]
        return jnp.concatenate(ext[:-2], axis=0), jnp.concatenate(ext[1:-1], axis=0)

    _ffn_hidden(h, wup_ref, cw_ref, cb_ref, shifted, act_ref)
    o_ref[...] = o_ref[...] + _dot(act_ref[...], wdn_ref[...])


def _s_ffn(x, c, m, wout, g, wup, cw, cb, wdn, st):
    args = (x, c, m, wout, g, wup, cw, cb, wdn, st)
    return pl.pallas_call(
        _s_ffn_kernel,
        grid=(1,),
        in_specs=[_const_spec(a.shape) for a in args],
        out_specs=[_const_spec(x.shape), _const_spec(st.shape)],
        out_shape=[jax.ShapeDtypeStruct(x.shape, F32), jax.ShapeDtypeStruct(st.shape, F32)],
        scratch_shapes=[pltpu.VMEM((x.shape[0], D_FF), BF16)],
        compiler_params=_params(1),
        name="s_ffn",
    )(*args)


def _mixb_project(x, gkv, wkv, kg, gmix, winb, qg, mqg):
    xn = x * lax.rsqrt(jnp.mean(x * x, axis=-1, keepdims=True) + EPS)
    kv = _dot((xn * gkv).astype(BF16), wkv)
    qm = _dot((xn * gmix).astype(BF16), winb)
    k = _group_rms(kv[:, :DIFF_W], kg, DIFF_HD)
    v = kv[:, DIFF_W:]
    q = _group_rms(qm[:, :DIFF_W], qg, DIFF_HD) * (DIFF_HD ** -0.5)
    mq = _group_rms(qm[:, DIFF_W:], mqg, MEM_HD) * (MEM_HD ** -0.5)
    return k, v, q, mq


def _p_mixb_kernel(x_ref, gkv_ref, wkv_ref, kg_ref, gmix_ref, winb_ref, qg_ref, mqg_ref, mk_ref, mv_ref,
                   k_ref, v_ref, kb_ref, vb_ref, qb_ref, mb_ref):
    k, v, q, mq = _mixb_project(x_ref[0], gkv_ref[...], wkv_ref[...], kg_ref[...], gmix_ref[...],
                                winb_ref[...], qg_ref[...], mqg_ref[...])
    k_ref[0] = k
    v_ref[0] = v
    kb_ref[0] = k.astype(BF16)
    vb_ref[0] = v.astype(BF16)
    qb_ref[0] = q.astype(BF16)
    m = _mem_attend_shared(mq, mk_ref[0, 0].astype(BF16), mv_ref[0, 0].astype(BF16))
    mb_ref[0] = m.astype(BF16)


def _p_mixb(x, gkv, wkv, kg, gmix, winb, qg, mqg, mem_k, mem_v):
    batch, seq, _ = x.shape
    tile = ROW_TILE
    tok = lambda w: pl.BlockSpec((1, tile, w), lambda b, t: (b, t, 0))
    mem = pl.BlockSpec((1, 1, MEM_LEN, MEM_W), lambda b, t: (1, b, 0, 0))
    shp = lambda w, dt: jax.ShapeDtypeStruct((batch, seq, w), dt)
    return pl.pallas_call(
        _p_mixb_kernel,
        grid=(batch, seq // tile),
        in_specs=[tok(D_MODEL)] + [_const_spec(a.shape) for a in (gkv, wkv, kg, gmix, winb, qg, mqg)]
        + [mem, mem],
        out_specs=[tok(DIFF_W)] * 5 + [tok(MEM_W)],
        out_shape=[shp(DIFF_W, F32), shp(DIFF_W, F32), shp(DIFF_W, BF16), shp(DIFF_W, BF16),
                   shp(DIFF_W, BF16), shp(MEM_W, BF16)],
        compiler_params=_params(2),
        name="p_mixb",
    )(x, gkv, wkv, kg, gmix, winb, qg, mqg, mem_k, mem_v)


def _s_mixb_kernel(x_ref, gkv_ref, wkv_ref, kg_ref, gmix_ref, winb_ref, qg_ref, mqg_ref,
                   k_ref, v_ref, q_ref, mq_ref):
    k, v, q, mq = _mixb_project(x_ref[...], gkv_ref[...], wkv_ref[...], kg_ref[...], gmix_ref[...],
                                winb_ref[...], qg_ref[...], mqg_ref[...])
    k_ref[...] = k
    v_ref[...] = v
    q_ref[...] = q
    mq_ref[...] = mq


def _s_mixb(x, gkv, wkv, kg, gmix, winb, qg, mqg):
    n = x.shape[0]
    args = (x, gkv, wkv, kg, gmix, winb, qg, mqg)
    widths = (DIFF_W, DIFF_W, DIFF_W, MEM_W)
    return pl.pallas_call(
        _s_mixb_kernel,
        grid=(1,),
        in_specs=[_const_spec(a.shape) for a in args],
        out_specs=[_const_spec((n, w)) for w in widths],
        out_shape=[jax.ShapeDtypeStruct((n, w), F32) for w in widths],
        compiler_params=_params(1),
        name="s_mixb",
    )(*args)


def _p_att_kernel(q_ref, k_ref, v_ref, m_ref, x_ref, lam_ref, sg_ref, wout_ref, o_ref, c_ref):
    tq = q_ref.shape[1]
    qi = pl.program_id(1)
    lam = _lambda(lam_ref[...])
    row = lax.broadcasted_iota(jnp.int32, (tq, tq), 0)
    col = lax.broadcasted_iota(jnp.int32, (tq, tq), 1)
    causal = col <= row
    first_map = lax.broadcasted_iota(jnp.int32, (tq, DIFF_VD), 1) < DIFF_HD

    for h in range(DIFF_HEADS):
        lanes = slice(h * DIFF_VD, (h + 1) * DIFF_VD)
        q = q_ref[0, :, lanes]
        zero = jnp.zeros_like(q)
        qs = (jnp.where(first_map, q, zero), jnp.where(first_map, zero, q))

        def block(j, carry, masked):
            start = pl.multiple_of(j * tq, tq)
            kb = k_ref[0, pl.ds(start, tq), lanes]
            vb = v_ref[0, pl.ds(start, tq), lanes]
            new = []
            for c in range(2):
                m_old, l_old, acc = carry[c]
                s = _dot_nt(qs[c], kb)
                if masked:
                    s = jnp.where(causal, s, NEG_INF)
                m_new = jnp.maximum(m_old, jnp.max(s, axis=-1, keepdims=True))
                alpha = jnp.exp(m_old - m_new)
                p = jnp.exp(s - m_new)
                new.append((m_new, alpha * l_old + jnp.sum(p, axis=-1, keepdims=True),
                            alpha * acc + _dot(p.astype(BF16), vb)))
            return tuple(new)

        init = tuple((jnp.full((tq, 1), NEG_INF, F32), jnp.zeros((tq, 1), F32),
                      jnp.zeros((tq, DIFF_VD), F32)) for _ in range(2))
        pair = lambda i, c: block(2 * i + 1, block(2 * i, c, False), False)
        carry = lax.fori_loop(0, qi // 2, pair, init)
        carry = lax.cond(qi % 2 == 1, lambda c: block(qi - 1, c, False), lambda c: c, carry)
        (_, l1, a1), (_, l2, a2) = block(qi, carry, True)
        o = a1 / l1 - lam * (a2 / l2)
        o = _rms(o, sg_ref[:, lanes]) * (1.0 - LAM_INIT_B)
        c_ref[:, lanes] = o.astype(BF16)

    o_ref[0] = (x_ref[0] + _dot(c_ref[...], wout_ref[0:DIFF_W, :])
                + _dot(m_ref[0], wout_ref[DIFF_W:, :]))


def _p_att(q, k, v, m, x, lam, sg, wout):
    batch, seq, _ = x.shape
    tile = ROW_TILE
    tok = lambda w: pl.BlockSpec((1, tile, w), lambda b, t: (b, t, 0))
    kv = pl.BlockSpec((1, seq, DIFF_W), lambda b, t: (b, 0, 0), pipeline_mode=pl.Buffered(1))
    return pl.pallas_call(
        _p_att_kernel,
        grid=(batch, seq // tile),
        in_specs=[tok(DIFF_W), kv, kv, tok(MEM_W), tok(D_MODEL), _const_spec(lam.shape),
                  _const_spec(sg.shape), _const_spec(wout.shape)],
        out_specs=tok(D_MODEL),
        out_shape=jax.ShapeDtypeStruct(x.shape, F32),
        scratch_shapes=[pltpu.VMEM((tile, DIFF_W), BF16)],
        compiler_params=_params(2),
        name="p_att",
    )(q, k, v, m, x, lam, sg, wout)


def _s_mixa_kernel(x_ref, st_ref, gmix_ref, win_ref, cw_ref, cb_ref, lng_ref, lnb_ref, mqg_ref,
                   glu_ref, c_ref, mq_ref):
    n_t, nb, _ = x_ref.shape
    n_hist = st_ref.shape[0]
    x = x_ref[...].reshape(n_t * nb, D_MODEL)
    h = _rms(x, gmix_ref[...]).astype(BF16)
    z = _dot(h, win_ref[...])
    glu = z[:, :C_A] * jax.nn.sigmoid(z[:, C_A:2 * C_A])
    glu_ref[...] = glu.reshape(n_t, nb, C_A)
    mq = _group_rms(z[:, 2 * C_A:], mqg_ref[...], MEM_HD) * (MEM_HD ** -0.5)
    mq_ref[...] = mq.reshape(n_t, nb, MEM_W)

    ys = [[] for _ in range(n_t)]
    for l0 in range(0, C_A, CONV_LANES):
        lanes = slice(l0, l0 + CONV_LANES)
        acc = [jnp.broadcast_to(cb_ref[:, lanes], (nb, CONV_LANES)) for _ in range(n_t)]
        for r in range(n_hist + n_t):
            e = st_ref[r, :, lanes] if r < n_hist else glu[(r - n_hist) * nb:(r - n_hist + 1) * nb, lanes]
            for t in range(n_t):
                if 0 <= r - t < CONV_A_WIDTH:
                    acc[t] = acc[t] + cw_ref[r - t:r - t + 1, lanes] * e
        for t in range(n_t):
            ys[t].append(acc[t])
    for t in range(n_t):
        y = jnp.concatenate(ys[t], axis=-1)
        c_ref[t] = _silu(_layernorm(y, lng_ref[...], lnb_ref[...])).astype(BF16)


def _s_mixa(x_tm, st_tm, gmix, win, cw, cb, lng, lnb, mqg):
    n_t, n_seq, _ = x_tm.shape
    nb = 32
    blk = lambda r, w: pl.BlockSpec((r, nb, w), lambda i: (0, i, 0))
    consts = (gmix, win, cw, cb, lng, lnb, mqg)
    return pl.pallas_call(
        _s_mixa_kernel,
        grid=(n_seq // nb,),
        in_specs=[blk(n_t, D_MODEL), blk(st_tm.shape[0], C_A)] + [_const_spec(a.shape) for a in consts],
        out_specs=[blk(n_t, C_A), blk(n_t, C_A), blk(n_t, MEM_W)],
        out_shape=[jax.ShapeDtypeStruct((n_t, n_seq, C_A), F32),
                   jax.ShapeDtypeStruct((n_t, n_seq, C_A), BF16),
                   jax.ShapeDtypeStruct((n_t, n_seq, MEM_W), F32)],
        compiler_params=_params(1),
        name="s_mixa",
    )(x_tm, st_tm, *consts)


def _s_mem_kernel(q_ref, mk_ref, mv_ref, o_ref):
    nb = q_ref.shape[0]
    rows = MEM_HEADS * PAD_T
    head_of_row = jnp.right_shift(lax.broadcasted_iota(jnp.int32, (rows, MEM_W), 0), 3)
    own = head_of_row == _lane_group((rows, MEM_W), 6)
    head_of_lane = _lane_group((PAD_T, MEM_W), 6)
    for b in range(nb):
        q = q_ref[b]
        qb = jnp.where(own, jnp.concatenate([q] * MEM_HEADS, axis=0), 0.0).astype(BF16)
        s = _dot(qb, mk_ref[0, b].astype(BF16))
        p = jnp.exp(s - jnp.max(s, axis=-1, keepdims=True))
        o = _dot_nt(p.astype(BF16), mv_ref[0, b].astype(BF16)) / jnp.sum(p, axis=-1, keepdims=True)
        out = jnp.zeros((PAD_T, MEM_W), F32)
        for h in range(MEM_HEADS):
            out = jnp.where(head_of_lane == h, o[h * PAD_T:(h + 1) * PAD_T], out)
        o_ref[b] = out


def _s_mem(q_pad, mem_k, mem_v, layer):
    n_seq = q_pad.shape[0]
    nb = 8
    tok = pl.BlockSpec((nb, PAD_T, MEM_W), lambda i: (i, 0, 0))
    mem = pl.BlockSpec((1, nb, MEM_LEN, MEM_W), lambda i: (layer, i, 0, 0))
    return pl.pallas_call(
        _s_mem_kernel,
        grid=(n_seq // nb,),
        in_specs=[tok, mem, mem],
        out_specs=tok,
        out_shape=jax.ShapeDtypeStruct(q_pad.shape, F32),
        compiler_params=_params(1),
        name="s_mem",
    )(q_pad, mem_k, mem_v)


def _page_copies(pt_ref, ck_hbm, cv_hbm, kbuf, vbuf, sem, seq, slot, n_pages):
    copies = []
    for p in range(n_pages):
        page = pt_ref[seq * n_pages + p]
        copies.append(pltpu.make_async_copy(ck_hbm.at[page], kbuf.at[slot, p], sem.at[0, slot]))
        copies.append(pltpu.make_async_copy(cv_hbm.at[page], vbuf.at[slot, p], sem.at[1, slot]))
    return copies


def _s_paged_kernel(pt_ref, q_ref, kn_ref, vn_ref, lam_ref, sg_ref, ck_hbm, cv_hbm, o_ref,
                    kbuf, vbuf, sem, *, n_pages, n_new):
    seq = pl.program_id(0)
    slot = seq % 2
    fetch = functools.partial(_page_copies, pt_ref, ck_hbm, cv_hbm, kbuf, vbuf, sem, n_pages=n_pages)

    @pl.when(seq == 0)
    def _():
        for c in fetch(seq=0, slot=0):
            c.start()

    @pl.when(seq + 1 < pl.num_programs(0))
    def _():
        for c in fetch(seq=seq + 1, slot=1 - slot):
            c.start()

    for c in fetch(seq=seq, slot=slot):
        c.wait()
    k_refs = [kbuf.at[slot, p] for p in range(n_pages)]
    v_refs = [vbuf.at[slot, p] for p in range(n_pages)]
    lam = _lambda(lam_ref[...])
    first_map = lax.broadcasted_iota(jnp.int32, (PAD_T, DIFF_VD), 1) < DIFF_HD
    t_q = jnp.bitwise_and(lax.broadcasted_iota(jnp.int32, (2 * PAD_T, 1), 0), PAD_T - 1)

    outs = []
    for h in range(DIFF_HEADS):
        lanes = slice(h * DIFF_VD, (h + 1) * DIFF_VD)
        q = q_ref[0, :, lanes]
        qf = jnp.concatenate([jnp.where(first_map, q, 0.0), jnp.where(first_map, 0.0, q)], axis=0)
        qb = qf.astype(BF16)

        s_past = [_dot_nt(qb, k_refs[p][h].astype(BF16)) for p in range(n_pages)]
        s_new = [jnp.where(t <= t_q,
                           jnp.sum(qf * kn_ref[0, t:t + 1, lanes], axis=-1, keepdims=True), NEG_INF)
                 for t in range(n_new)]

        m_el = s_past[0]
        for s in s_past[1:]:
            m_el = jnp.maximum(m_el, s)
        m = jnp.max(m_el, axis=-1, keepdims=True)
        for s in s_new:
            m = jnp.maximum(m, s)

        p_past = [jnp.exp(s - m) for s in s_past]
        p_new = [jnp.exp(s - m) for s in s_new]
        l_el = p_past[0]
        for p in p_past[1:]:
            l_el = l_el + p
        l = jnp.sum(l_el, axis=-1, keepdims=True)
        for p in p_new:
            l = l + p
        w1 = 1.0 / l[:PAD_T]
        w2 = lam / l[PAD_T:]

        acc = jnp.zeros((PAD_T, DIFF_VD), F32)
        for i in range(n_pages):
            a = (p_past[i][:PAD_T] * w1 - p_past[i][PAD_T:] * w2).astype(BF16)
            acc = acc + _dot(a, v_refs[i][h].astype(BF16))
        for t in range(n_new):
            acc = acc + (p_new[t][:PAD_T] * w1 - p_new[t][PAD_T:] * w2) * vn_ref[0, t:t + 1, lanes]
        outs.append(_rms(acc, sg_ref[:, lanes]) * (1.0 - LAM_INIT_B))
    o_ref[0] = jnp.concatenate(outs, axis=-1)


def _s_paged(page_table, q_pad, k_new, v_new, lam, sg, cache_k, cache_v):
    n_seq, n_pages = page_table.shape
    n_new = k_new.shape[1]
    tok = lambda r: pl.BlockSpec((1, r, DIFF_W), lambda b, pt: (b, 0, 0))
    hbm = pl.BlockSpec(memory_space=pl.ANY)
    page_buf = pltpu.VMEM((2, n_pages, DIFF_HEADS, PAGE_SIZE, DIFF_VD), F32)
    grid_spec = pltpu.PrefetchScalarGridSpec(
        num_scalar_prefetch=1,
        grid=(n_seq,),
        in_specs=[tok(PAD_T), tok(n_new), tok(n_new), _const_spec(lam.shape), _const_spec(sg.shape),
                  hbm, hbm],
        out_specs=tok(PAD_T),
        scratch_shapes=[page_buf, page_buf, pltpu.SemaphoreType.DMA((2, 2))],
    )
    return pl.pallas_call(
        functools.partial(_s_paged_kernel, n_pages=n_pages, n_new=n_new),
        grid_spec=grid_spec,
        out_shape=jax.ShapeDtypeStruct(q_pad.shape, F32),
        compiler_params=_params(1),
        name="s_paged",
    )(page_table.reshape(-1), q_pad, k_new, v_new, lam, sg, cache_k, cache_v)


def _to_time_major(a):
    return jnp.swapaxes(a, 0, 1)


def _pad_tokens(a_tm):
    a = jnp.swapaxes(a_tm, 0, 1)
    return jnp.pad(a, ((0, 0), (0, PAD_T - a.shape[1]), (0, 0)))


def _from_padded(a, n_t, dtype):
    return jnp.swapaxes(a[:, :n_t], 0, 1).astype(dtype)


def kernel(x_prompt, x_sample, mem_prompt, cache_mem_k, cache_mem_v, state_conv_a, state_ffn_conv, cache_k, cache_v, page_table, norm_mix, norm_ffn, norm_mem, w_mem_kv, mem_q_norm, mem_k_norm, w_in_a, conv_a_w, conv_a_b, ln_a_g, ln_a_b, w_out_a, norm_kv, w_kv, k_norm, w_in_b, q_norm, lambdas, subln_g, w_out_b, w_up, ffn_conv_w, ffn_conv_b, w_down):
    batch, seq, _ = x_prompt.shape
    n_seq, n_t, _ = x_sample.shape
    depth = norm_mix.shape[0]
    assert depth == 2 and w_in_a.shape[0] == 1 and w_in_b.shape[0] == 1
    assert seq % ROW_TILE == 0 and n_t <= PAD_T and n_seq % 32 == 0

    row = lambda a: a.reshape(1, -1)
    gmix = [row(norm_mix[l]) for l in range(depth)]
    gffn = [row(norm_ffn[l]) for l in range(depth)]
    mqg = [row(jnp.tile(mem_q_norm[l], MEM_HEADS)) for l in range(depth)]
    mkg = jnp.tile(mem_k_norm, (1, MEM_HEADS)).reshape(depth, 1, MEM_W)
    kg = row(jnp.tile(k_norm.reshape(-1), DIFF_HEADS))
    qg = row(jnp.tile(q_norm[0].reshape(-1), DIFF_HEADS))
    sg = row(jnp.tile(subln_g[0], DIFF_HEADS))
    lam = lambdas[0]
    win_a = w_in_a[0].astype(BF16)
    wout_a = w_out_a[0].astype(BF16)
    wkv = w_kv.astype(BF16)
    win_b = w_in_b[0].astype(BF16)
    wout_b = w_out_b[0].astype(BF16)
    wup = [w_up[l].astype(BF16) for l in range(depth)]
    wdn = [w_down[l].astype(BF16) for l in range(depth)]
    cw_a, cb_a = conv_a_w[0], row(conv_a_b[0])
    lng, lnb = row(ln_a_g[0]), row(ln_a_b[0])
    fcw = [ffn_conv_w[l] for l in range(depth)]
    fcb = [row(ffn_conv_b[l]) for l in range(depth)]

    p_mem_k, p_mem_v = _mem_kv(mem_prompt, norm_mem.reshape(depth, 1, D_MODEL), w_mem_kv.astype(BF16), mkg)
    x1, conv_tail = _p_mixa(x_prompt, gmix[0], win_a, cw_a, cb_a, lng, lnb, mqg[0], p_mem_k, p_mem_v, wout_a)
    x2, p_ffn0 = _p_ffn(x1, gffn[0], wup[0], fcw[0], fcb[0], wdn[0])
    p_k, p_v, kb, vb, qb, mb = _p_mixb(x2, row(norm_kv), wkv, kg, gmix[1], win_b, qg, mqg[1], p_mem_k, p_mem_v)
    x3 = _p_att(qb, kb, vb, mb, x2, lam, sg, wout_b)
    y_prompt, p_ffn1 = _p_ffn(x3, gffn[1], wup[1], fcw[1], fcb[1], wdn[1])

    hd = (MEM_HEADS, MEM_HD)
    p_mem_k = p_mem_k.reshape(depth, batch, MEM_LEN, *hd)
    p_mem_v = p_mem_v.reshape(depth, batch, MEM_LEN, *hd)
    p_conv_a = conv_tail[None, :, CARRY_ROWS - (CONV_A_WIDTH - 1):, :]
    p_ffn_conv = jnp.stack([p_ffn0, p_ffn1])
    p_k = p_k.reshape(batch, seq, DIFF_HEADS, DIFF_VD)
    p_v = p_v.reshape(batch, seq, DIFF_HEADS, DIFF_VD)

    smem_k = jnp.transpose(cache_mem_k, (0, 1, 3, 4, 2)).reshape(depth, n_seq, MEM_W, MEM_LEN)
    smem_v = jnp.transpose(cache_mem_v, (0, 1, 3, 4, 2)).reshape(depth, n_seq, MEM_W, MEM_LEN)
    xs = _to_time_major(x_sample)
    glu, c_a, mq_a = _s_mixa(xs, _to_time_major(state_conv_a[0]), gmix[0], win_a, cw_a, cb_a, lng, lnb, mqg[0])
    m_a = _from_padded(_s_mem(_pad_tokens(mq_a), smem_k, smem_v, 0), n_t, BF16)
    flat = lambda a: a.reshape(n_t * n_seq, a.shape[-1])
    xs2, s_ffn0 = _s_ffn(flat(xs), flat(c_a), flat(m_a), wout_a, gffn[0], wup[0], fcw[0], fcb[0], wdn[0],
                         _to_time_major(state_ffn_conv[0]))
    k_new, v_new, q_s, mq_b = _s_mixb(xs2, row(norm_kv), wkv, kg, gmix[1], win_b, qg, mqg[1])
    unflat = lambda a: a.reshape(n_t, n_seq, a.shape[-1])
    m_b = _from_padded(_s_mem(_pad_tokens(unflat(mq_b)), smem_k, smem_v, 1), n_t, BF16)
    k_new_sm = jnp.swapaxes(unflat(k_new), 0, 1)
    v_new_sm = jnp.swapaxes(unflat(v_new), 0, 1)
    c_b = _s_paged(page_table, _pad_tokens(unflat(q_s)), k_new_sm, v_new_sm, lam, sg,
                   jnp.swapaxes(cache_k, 1, 2), jnp.swapaxes(cache_v, 1, 2))
    c_b = _from_padded(c_b, n_t, BF16)
    ys, s_ffn1 = _s_ffn(xs2, flat(c_b), flat(m_b), wout_b, gffn[1], wup[1], fcw[1], fcb[1], wdn[1],
                        _to_time_major(state_ffn_conv[1]))

    y_sample = jnp.swapaxes(unflat(ys), 0, 1)
    s_conv_a = jnp.concatenate([state_conv_a[0][:, n_t:], jnp.swapaxes(glu, 0, 1)], axis=1)[None]
    s_ffn_conv = jnp.stack([_to_time_major(s_ffn0), _to_time_major(s_ffn1)])
    s_k = k_new_sm.reshape(n_seq, n_t, DIFF_HEADS, DIFF_VD)
    s_v = v_new_sm.reshape(n_seq, n_t, DIFF_HEADS, DIFF_VD)
    return (y_prompt, y_sample, p_mem_k, p_mem_v, p_conv_a, p_ffn_conv, p_k, p_v,
            s_conv_a, s_ffn_conv, s_k, s_v)
```

```python
import functools
import math

import jax
import jax.numpy as jnp
from jax import lax
from jax.experimental import pallas as pl
from jax.experimental.pallas import tpu as pltpu

F32 = jnp.float32
BF16 = jnp.bfloat16

D_MODEL = 1024
MEM_LEN = 256
MEM_HEADS = 4
MEM_W = 256
MEM_HD = 64
C_A = 768
CONV_A_WIDTH = 31
DIFF_HD = 64
DIFF_VD = 128
DIFF_HEADS = 6
DIFF_W = 768
D_FF = 2816
PAGE_SIZE = 128
EPS = 1e-6
LN_EPS = 1e-5
NEG_INF = -1e30
LAM_INIT_B = 0.8 - 0.6 * math.exp(-0.3 * 1)

FF_CHUNK = 256
N_FF_CHUNKS = D_FF // FF_CHUNK
ROW_TILE = 512
CONV_ROWS = 128
CONV_LANES = 256
CARRY_ROWS = 32
ATT_HEADS_PER_TRIP = 3
PAD_T = 8
VMEM_LIMIT = 56 * 1024 * 1024


def _const_spec(shape):
    n = len(shape)
    return pl.BlockSpec(shape, lambda *_: (0,) * n, pipeline_mode=pl.Buffered(1))


def _params(n_grid):
    return pltpu.CompilerParams(dimension_semantics=("arbitrary",) * n_grid,
                                vmem_limit_bytes=VMEM_LIMIT)


def _dot(a, b):
    return jnp.dot(a, b, preferred_element_type=F32)


def _dot_nt(a, b):
    return lax.dot_general(a, b, (((1,), (1,)), ((), ())), preferred_element_type=F32)


def _rms(x, g):
    return x * lax.rsqrt(jnp.mean(x * x, axis=-1, keepdims=True) + EPS) * g


def _group_rms(x, g, group):
    width = x.shape[-1]
    blk = 256
    r = lax.broadcasted_iota(jnp.int32, (blk, blk), 0)
    c = lax.broadcasted_iota(jnp.int32, (blk, blk), 1)
    shift = group.bit_length() - 1
    ones = (jnp.right_shift(r, shift) == jnp.right_shift(c, shift)).astype(BF16)
    outs = []
    for i in range(width // blk):
        xc = x[:, i * blk:(i + 1) * blk]
        ss = _dot((xc * xc).astype(BF16), ones)
        outs.append(xc * lax.rsqrt(ss * (1.0 / group) + EPS) * g[:, i * blk:(i + 1) * blk])
    return outs[0] if len(outs) == 1 else jnp.concatenate(outs, axis=-1)


def _layernorm(x, g, b):
    xc = x - jnp.mean(x, axis=-1, keepdims=True)
    var = jnp.mean(xc * xc, axis=-1, keepdims=True)
    return xc * lax.rsqrt(var + LN_EPS) * g + b


def _silu(x):
    return x * jax.nn.sigmoid(x)


def _lane_group(shape, shift):
    return jnp.right_shift(lax.broadcasted_iota(jnp.int32, shape, len(shape) - 1), shift)


def _mem_attend_shared(mq, mk, mv):
    head_of_lane = _lane_group(mq.shape, 6)
    out = jnp.zeros(mq.shape, F32)
    for h in range(MEM_HEADS):
        sel = head_of_lane == h
        s = _dot_nt(jnp.where(sel, mq, 0.0).astype(BF16), mk)
        p = jnp.exp(s - jnp.max(s, axis=-1, keepdims=True))
        o = _dot(p.astype(BF16), mv) / jnp.sum(p, axis=-1, keepdims=True)
        out = jnp.where(sel, o, out)
    return out


def _lambda(lf):
    s1 = jnp.sum(lf[0:1] * lf[1:2], axis=-1, keepdims=True)
    s2 = jnp.sum(lf[2:3] * lf[3:4], axis=-1, keepdims=True)
    return jnp.exp(s1) - jnp.exp(s2) + LAM_INIT_B


def _mem_kv_kernel(mem_ref, g_ref, w_ref, kg_ref, k_ref, v_ref):
    h = _rms(mem_ref[0], g_ref[0]).astype(BF16)
    kv = _dot(h, w_ref[0])
    k_ref[0, 0] = _group_rms(kv[:, :MEM_W], kg_ref[0], MEM_HD)
    v_ref[0, 0] = kv[:, MEM_W:]


def _mem_kv(mem, norm_mem, w_mem_kv, mem_k_gain):
    depth, batch = w_mem_kv.shape[0], mem.shape[0]
    out = jax.ShapeDtypeStruct((depth, batch, MEM_LEN, MEM_W), F32)
    return pl.pallas_call(
        _mem_kv_kernel,
        grid=(depth, batch),
        in_specs=[
            pl.BlockSpec((1, MEM_LEN, D_MODEL), lambda l, b: (b, 0, 0)),
            pl.BlockSpec((1, 1, D_MODEL), lambda l, b: (l, 0, 0)),
            pl.BlockSpec((1, D_MODEL, 2 * MEM_W), lambda l, b: (l, 0, 0)),
            pl.BlockSpec((1, 1, MEM_W), lambda l, b: (l, 0, 0)),
        ],
        out_specs=[pl.BlockSpec((1, 1, MEM_LEN, MEM_W), lambda l, b: (l, b, 0, 0))] * 2,
        out_shape=[out, out],
        compiler_params=_params(2),
        name="mem_kv",
    )(mem, norm_mem, w_mem_kv, mem_k_gain)


def _conv31_tile(ext_ref, cw_ref, cb_ref, r0, lanes):
    span = CONV_ROWS + CARRY_ROWS + 8
    e = ext_ref[r0:r0 + span, lanes]
    acc = jnp.broadcast_to(cb_ref[:, lanes], (CONV_ROWS, CONV_LANES))
    for r in range(8):
        u = e if r == 0 else pltpu.roll(e, span - r, 0)
        for o in range(2, CONV_A_WIDTH + 2):
            if o % 8 == r:
                a = o // 8
                acc = acc + cw_ref[o - 2:o - 1, lanes] * u[8 * a:8 * a + CONV_ROWS]
    return acc


def _p_mixa_kernel(x_ref, gmix_ref, win_ref, cw_ref, cb_ref, lng_ref, lnb_ref, mqg_ref, mk_ref, mv_ref,
                   wout_ref, x1_ref, st_ref, ext_ref, cm_ref):
    tile = x_ref.shape[1]

    @pl.when(pl.program_id(1) == 0)
    def _():
        ext_ref[0:CARRY_ROWS, :] = jnp.zeros((CARRY_ROWS, C_A), F32)
        ext_ref[CARRY_ROWS + tile:CARRY_ROWS + tile + 8, :] = jnp.zeros((8, C_A), F32)

    x = x_ref[0]
    h = _rms(x, gmix_ref[...]).astype(BF16)
    z = _dot(h, win_ref[...])
    ext_ref[CARRY_ROWS:CARRY_ROWS + tile, :] = z[:, :C_A] * jax.nn.sigmoid(z[:, C_A:2 * C_A])

    for r0 in range(0, tile, CONV_ROWS):
        y = jnp.concatenate(
            [_conv31_tile(ext_ref, cw_ref, cb_ref, r0, slice(l0, l0 + CONV_LANES))
             for l0 in range(0, C_A, CONV_LANES)], axis=-1)
        c = _silu(_layernorm(y, lng_ref[...], lnb_ref[...]))
        cm_ref[r0:r0 + CONV_ROWS, 0:C_A] = c.astype(BF16)

    st_ref[0] = ext_ref[tile:tile + CARRY_ROWS, :]
    ext_ref[0:CARRY_ROWS, :] = ext_ref[tile:tile + CARRY_ROWS, :]

    mq = _group_rms(z[:, 2 * C_A:], mqg_ref[...], MEM_HD) * (MEM_HD ** -0.5)
    m = _mem_attend_shared(mq, mk_ref[0, 0].astype(BF16), mv_ref[0, 0].astype(BF16))
    cm_ref[:, C_A:] = m.astype(BF16)
    x1_ref[0] = x + _dot(cm_ref[...], wout_ref[...])


def _p_mixa(x, gmix, win, cw, cb, lng, lnb, mqg, mem_k, mem_v, wout):
    batch, seq, _ = x.shape
    tile = ROW_TILE
    tok = pl.BlockSpec((1, tile, D_MODEL), lambda b, t: (b, t, 0))
    mem = pl.BlockSpec((1, 1, MEM_LEN, MEM_W), lambda b, t: (0, b, 0, 0))
    return pl.pallas_call(
        _p_mixa_kernel,
        grid=(batch, seq // tile),
        in_specs=[tok, _const_spec(gmix.shape), _const_spec(win.shape), _const_spec(cw.shape),
                  _const_spec(cb.shape), _const_spec(lng.shape), _const_spec(lnb.shape),
                  _const_spec(mqg.shape), mem, mem, _const_spec(wout.shape)],
        out_specs=[tok, pl.BlockSpec((1, CARRY_ROWS, C_A), lambda b, t: (b, 0, 0))],
        out_shape=[jax.ShapeDtypeStruct(x.shape, F32),
                   jax.ShapeDtypeStruct((batch, CARRY_ROWS, C_A), F32)],
        scratch_shapes=[pltpu.VMEM((CARRY_ROWS + tile + 8, C_A), F32),
                        pltpu.VMEM((tile, D_MODEL), BF16)],
        compiler_params=_params(2),
        name="p_mixa",
    )(x, gmix, win, cw, cb, lng, lnb, mqg, mem_k, mem_v, wout)


def _ffn_hidden(h, wup_ref, cw_ref, cb_ref, conv_fn, act_ref):
    for j in range(N_FF_CHUNKS):
        lanes = slice(j * FF_CHUNK, (j + 1) * FF_CHUNK)
        gate = _dot(h, wup_ref[:, lanes])
        val = _dot(h, wup_ref[:, D_FF + j * FF_CHUNK:D_FF + (j + 1) * FF_CHUNK])
        g2, g1 = conv_fn(lanes, gate)
        conv = (cw_ref[0:1, lanes] * g2 + cw_ref[1:2, lanes] * g1 + cw_ref[2:3, lanes] * gate
                + cb_ref[:, lanes])
        act_ref[:, lanes] = (jax.nn.gelu(conv) * val).astype(BF16)


def _p_ffn_kernel(x_ref, g_ref, wup_ref, cw_ref, cb_ref, wdn_ref, o_ref, st_ref, carry_ref, act_ref):
    tile = x_ref.shape[1]

    @pl.when(pl.program_id(1) == 0)
    def _():
        carry_ref[...] = jnp.zeros(carry_ref.shape, F32)

    x = x_ref[0]
    h = _rms(x, g_ref[...]).astype(BF16)
    row = lax.broadcasted_iota(jnp.int32, (tile, FF_CHUNK), 0)

    def shifted(lanes, gate):
        c2 = carry_ref[0:1, lanes]
        c1 = carry_ref[1:2, lanes]
        g1 = jnp.where(row == 0, c1, pltpu.roll(gate, 1, 0))
        g2 = jnp.where(row == 0, c2, jnp.where(row == 1, c1, pltpu.roll(gate, 2, 0)))
        carry_ref[0:2, lanes] = gate[tile - 2:tile]
        return g2, g1

    _ffn_hidden(h, wup_ref, cw_ref, cb_ref, shifted, act_ref)
    o_ref[0] = x + _dot(act_ref[...], wdn_ref[...])
    st_ref[0] = carry_ref[0:2, :]


def _p_ffn(x, g, wup, cw, cb, wdn):
    batch, seq, _ = x.shape
    tile = ROW_TILE
    tok = pl.BlockSpec((1, tile, D_MODEL), lambda b, t: (b, t, 0))
    return pl.pallas_call(
        _p_ffn_kernel,
        grid=(batch, seq // tile),
        in_specs=[tok, _const_spec(g.shape), _const_spec(wup.shape), _const_spec(cw.shape),
                  _const_spec(cb.shape), _const_spec(wdn.shape)],
        out_specs=[tok, pl.BlockSpec((1, 2, D_FF), lambda b, t: (b, 0, 0))],
        out_shape=[jax.ShapeDtypeStruct(x.shape, F32), jax.ShapeDtypeStruct((batch, 2, D_FF), F32)],
        scratch_shapes=[pltpu.VMEM((8, D_FF), F32), pltpu.VMEM((tile, D_FF), BF16)],
        compiler_params=_params(2),
        name="p_ffn",
    )(x, g, wup, cw, cb, wdn)


def _s_ffn_kernel(x_ref, c_ref, m_ref, wout_ref, g_ref, wup_ref, cw_ref, cb_ref, wdn_ref, st_ref,
                  o_ref, nst_ref, act_ref):
    n_seq = st_ref.shape[1]
    o_ref[...] = (x_ref[...] + _dot(c_ref[...], wout_ref[0:DIFF_W, :])
                  + _dot(m_ref[...], wout_ref[DIFF_W:, :]))
    h = _rms(o_ref[...], g_ref[...]).astype(BF16)

    def shifted(lanes, gate):
        ext = [st_ref[0, :, lanes], st_ref[1, :, lanes]] + [
            gate[t * n_seq:(t + 1) * n_seq] for t in range(gate.shape[0] // n_seq)]
        nst_ref[0, :, lanes] = ext[-2]
        nst_ref[1, :, lanes] = ext[-1]
        return jnp.concatenate(ext[:-2], axis=0), jnp.concatenate(ext[1:-1], axis=0)

    _ffn_hidden(h, wup_ref, cw_ref, cb_ref, shifted, act_ref)
    o_ref[...] = o_ref[...] + _dot(act_ref[...], wdn_ref[...])


def _s_ffn(x, c, m, wout, g, wup, cw, cb, wdn, st):
    args = (x, c, m, wout, g, wup, cw, cb, wdn, st)
    return pl.pallas_call(
        _s_ffn_kernel,
        grid=(1,),
        in_specs=[_const_spec(a.shape) for a in args],
        out_specs=[_const_spec(x.shape), _const_spec(st.shape)],
        out_shape=[jax.ShapeDtypeStruct(x.shape, F32), jax.ShapeDtypeStruct(st.shape, F32)],
        scratch_shapes=[pltpu.VMEM((x.shape[0], D_FF), BF16)],
        compiler_params=_params(1),
        name="s_ffn",
    )(*args)


def _mixb_project(x, gkv, wkv, kg, gmix, winb, qg, mqg):
    xn = x * lax.rsqrt(jnp.mean(x * x, axis=-1, keepdims=True) + EPS)
    kv = _dot((xn * gkv).astype(BF16), wkv)
    qm = _dot((xn * gmix).astype(BF16), winb)
    k = _group_rms(kv[:, :DIFF_W], kg, DIFF_HD)
    v = kv[:, DIFF_W:]
    q = _group_rms(qm[:, :DIFF_W], qg, DIFF_HD) * (DIFF_HD ** -0.5)
    mq = _group_rms(qm[:, DIFF_W:], mqg, MEM_HD) * (MEM_HD ** -0.5)
    return k, v, q, mq


def _p_mixb_kernel(x_ref, gkv_ref, wkv_ref, kg_ref, gmix_ref, winb_ref, qg_ref, mqg_ref, mk_ref, mv_ref,
                   k_ref, v_ref, kb_ref, vb_ref, qb_ref, mb_ref):
    k, v, q, mq = _mixb_project(x_ref[0], gkv_ref[...], wkv_ref[...], kg_ref[...], gmix_ref[...],
                                winb_ref[...], qg_ref[...], mqg_ref[...])
    k_ref[0] = k
    v_ref[0] = v
    kb_ref[0] = k.astype(BF16)
    vb_ref[0] = v.astype(BF16)
    qb_ref[0] = q.astype(BF16)
    m = _mem_attend_shared(mq, mk_ref[0, 0].astype(BF16), mv_ref[0, 0].astype(BF16))
    mb_ref[0] = m.astype(BF16)


def _p_mixb(x, gkv, wkv, kg, gmix, winb, qg, mqg, mem_k, mem_v):
    batch, seq, _ = x.shape
    tile = ROW_TILE
    tok = lambda w: pl.BlockSpec((1, tile, w), lambda b, t: (b, t, 0))
    mem = pl.BlockSpec((1, 1, MEM_LEN, MEM_W), lambda b, t: (1, b, 0, 0))
    shp = lambda w, dt: jax.ShapeDtypeStruct((batch, seq, w), dt)
    return pl.pallas_call(
        _p_mixb_kernel,
        grid=(batch, seq // tile),
        in_specs=[tok(D_MODEL)] + [_const_spec(a.shape) for a in (gkv, wkv, kg, gmix, winb, qg, mqg)]
        + [mem, mem],
        out_specs=[tok(DIFF_W)] * 5 + [tok(MEM_W)],
        out_shape=[shp(DIFF_W, F32), shp(DIFF_W, F32), shp(DIFF_W, BF16), shp(DIFF_W, BF16),
                   shp(DIFF_W, BF16), shp(MEM_W, BF16)],
        compiler_params=_params(2),
        name="p_mixb",
    )(x, gkv, wkv, kg, gmix, winb, qg, mqg, mem_k, mem_v)


def _s_mixb_kernel(x_ref, gkv_ref, wkv_ref, kg_ref, gmix_ref, winb_ref, qg_ref, mqg_ref,
                   k_ref, v_ref, q_ref, mq_ref):
    k, v, q, mq = _mixb_project(x_ref[...], gkv_ref[...], wkv_ref[...], kg_ref[...], gmix_ref[...],
                                winb_ref[...], qg_ref[...], mqg_ref[...])
    k_ref[...] = k
    v_ref[...] = v
    q_ref[...] = q
    mq_ref[...] = mq


def _s_mixb(x, gkv, wkv, kg, gmix, winb, qg, mqg):
    n = x.shape[0]
    args = (x, gkv, wkv, kg, gmix, winb, qg, mqg)
    widths = (DIFF_W, DIFF_W, DIFF_W, MEM_W)
    return pl.pallas_call(
        _s_mixb_kernel,
        grid=(1,),
        in_specs=[_const_spec(a.shape) for a in args],
        out_specs=[_const_spec((n, w)) for w in widths],
        out_shape=[jax.ShapeDtypeStruct((n, w), F32) for w in widths],
        compiler_params=_params(1),
        name="s_mixb",
    )(*args)


def _p_att_kernel(q_ref, k_ref, v_ref, m_ref, x_ref, lam_ref, sg_ref, wout_ref, o_ref, c_ref):
    tq = q_ref.shape[1]
    qi = pl.program_id(1)
    lam = _lambda(lam_ref[...])
    row = lax.broadcasted_iota(jnp.int32, (tq, tq), 0)
    col = lax.broadcasted_iota(jnp.int32, (tq, tq), 1)
    causal = col <= row
    first_map = lax.broadcasted_iota(jnp.int32, (tq, DIFF_VD), 1) < DIFF_HD

    for h0 in range(0, DIFF_HEADS, ATT_HEADS_PER_TRIP):
        lanes = [slice(h * DIFF_VD, (h + 1) * DIFF_VD) for h in range(h0, h0 + ATT_HEADS_PER_TRIP)]
        qs = []
        for ln in lanes:
            q = q_ref[0, :, ln]
            zero = jnp.zeros_like(q)
            qs.append((jnp.where(first_map, q, zero), jnp.where(first_map, zero, q)))

        def block(j, carry, masked):
            start = pl.multiple_of(j * tq, tq)
            new = []
            for i, ln in enumerate(lanes):
                kb = k_ref[0, pl.ds(start, tq), ln]
                vb = v_ref[0, pl.ds(start, tq), ln]
                for c in range(2):
                    m_old, l_old, acc = carry[2 * i + c]
                    s = _dot_nt(qs[i][c], kb)
                    if masked:
                        s = jnp.where(causal, s, NEG_INF)
                    m_new = jnp.maximum(m_old, jnp.max(s, axis=-1, keepdims=True))
                    alpha = jnp.exp(m_old - m_new)
                    p = jnp.exp(s - m_new)
                    new.append((m_new, alpha * l_old + jnp.sum(p, axis=-1, keepdims=True),
                                alpha * acc + _dot(p.astype(BF16), vb)))
            return tuple(new)

        init = tuple((jnp.full((tq, 1), NEG_INF, F32), jnp.zeros((tq, 1), F32),
                      jnp.zeros((tq, DIFF_VD), F32)) for _ in range(2 * ATT_HEADS_PER_TRIP))
        carry = lax.fori_loop(0, qi, lambda j, c: block(j, c, False), init)
        carry = block(qi, carry, True)
        for i, ln in enumerate(lanes):
            (_, l1, a1), (_, l2, a2) = carry[2 * i], carry[2 * i + 1]
            o = a1 / l1 - lam * (a2 / l2)
            o = _rms(o, sg_ref[:, ln]) * (1.0 - LAM_INIT_B)
            c_ref[:, ln] = o.astype(BF16)

    o_ref[0] = (x_ref[0] + _dot(c_ref[...], wout_ref[0:DIFF_W, :])
                + _dot(m_ref[0], wout_ref[DIFF_W:, :]))


def _p_att(q, k, v, m, x, lam, sg, wout):
    batch, seq, _ = x.shape
    tile = ROW_TILE
    tok = lambda w: pl.BlockSpec((1, tile, w), lambda b, t: (b, t, 0))
    kv = pl.BlockSpec((1, seq, DIFF_W), lambda b, t: (b, 0, 0), pipeline_mode=pl.Buffered(1))
    return pl.pallas_call(
        _p_att_kernel,
        grid=(batch, seq // tile),
        in_specs=[tok(DIFF_W), kv, kv, tok(MEM_W), tok(D_MODEL), _const_spec(lam.shape),
                  _const_spec(sg.shape), _const_spec(wout.shape)],
        out_specs=tok(D_MODEL),
        out_shape=jax.ShapeDtypeStruct(x.shape, F32),
        scratch_shapes=[pltpu.VMEM((tile, DIFF_W), BF16)],
        compiler_params=_params(2),
        name="p_att",
    )(q, k, v, m, x, lam, sg, wout)


def _s_mixa_kernel(x_ref, st_ref, gmix_ref, win_ref, cw_ref, cb_ref, lng_ref, lnb_ref, mqg_ref,
                   glu_ref, c_ref, mq_ref):
    n_t, nb, _ = x_ref.shape
    n_hist = st_ref.shape[0]
    x = x_ref[...].reshape(n_t * nb, D_MODEL)
    h = _rms(x, gmix_ref[...]).astype(BF16)
    z = _dot(h, win_ref[...])
    glu = z[:, :C_A] * jax.nn.sigmoid(z[:, C_A:2 * C_A])
    glu_ref[...] = glu.reshape(n_t, nb, C_A)
    mq = _group_rms(z[:, 2 * C_A:], mqg_ref[...], MEM_HD) * (MEM_HD ** -0.5)
    mq_ref[...] = mq.reshape(n_t, nb, MEM_W)

    ys = [[] for _ in range(n_t)]
    for l0 in range(0, C_A, CONV_LANES):
        lanes = slice(l0, l0 + CONV_LANES)
        acc = [jnp.broadcast_to(cb_ref[:, lanes], (nb, CONV_LANES)) for _ in range(n_t)]
        for r in range(n_hist + n_t):
            e = st_ref[r, :, lanes] if r < n_hist else glu[(r - n_hist) * nb:(r - n_hist + 1) * nb, lanes]
            for t in range(n_t):
                if 0 <= r - t < CONV_A_WIDTH:
                    acc[t] = acc[t] + cw_ref[r - t:r - t + 1, lanes] * e
        for t in range(n_t):
            ys[t].append(acc[t])
    for t in range(n_t):
        y = jnp.concatenate(ys[t], axis=-1)
        c_ref[t] = _silu(_layernorm(y, lng_ref[...], lnb_ref[...])).astype(BF16)


def _s_mixa(x_tm, st_tm, gmix, win, cw, cb, lng, lnb, mqg):
    n_t, n_seq, _ = x_tm.shape
    nb = 32
    blk = lambda r, w: pl.BlockSpec((r, nb, w), lambda i: (0, i, 0))
    consts = (gmix, win, cw, cb, lng, lnb, mqg)
    return pl.pallas_call(
        _s_mixa_kernel,
        grid=(n_seq // nb,),
        in_specs=[blk(n_t, D_MODEL), blk(st_tm.shape[0], C_A)] + [_const_spec(a.shape) for a in consts],
        out_specs=[blk(n_t, C_A), blk(n_t, C_A), blk(n_t, MEM_W)],
        out_shape=[jax.ShapeDtypeStruct((n_t, n_seq, C_A), F32),
                   jax.ShapeDtypeStruct((n_t, n_seq, C_A), BF16),
                   jax.ShapeDtypeStruct((n_t, n_seq, MEM_W), F32)],
        compiler_params=_params(1),
        name="s_mixa",
    )(x_tm, st_tm, *consts)


def _s_mem_kernel(q_ref, mk_ref, mv_ref, o_ref):
    nb = q_ref.shape[0]
    rows = MEM_HEADS * PAD_T
    head_of_row = jnp.right_shift(lax.broadcasted_iota(jnp.int32, (rows, MEM_W), 0), 3)
    own = head_of_row == _lane_group((rows, MEM_W), 6)
    head_of_lane = _lane_group((PAD_T, MEM_W), 6)
    for b in range(nb):
        q = q_ref[b]
        qb = jnp.where(own, jnp.concatenate([q] * MEM_HEADS, axis=0), 0.0).astype(BF16)
        s = _dot(qb, mk_ref[0, b].astype(BF16))
        p = jnp.exp(s - jnp.max(s, axis=-1, keepdims=True))
        o = _dot_nt(p.astype(BF16), mv_ref[0, b].astype(BF16)) / jnp.sum(p, axis=-1, keepdims=True)
        out = jnp.zeros((PAD_T, MEM_W), F32)
        for h in range(MEM_HEADS):
            out = jnp.where(head_of_lane == h, o[h * PAD_T:(h + 1) * PAD_T], out)
        o_ref[b] = out


def _s_mem(q_pad, mem_k, mem_v, layer):
    n_seq = q_pad.shape[0]
    nb = 8
    tok = pl.BlockSpec((nb, PAD_T, MEM_W), lambda i: (i, 0, 0))
    mem = pl.BlockSpec((1, nb, MEM_LEN, MEM_W), lambda i: (layer, i, 0, 0))
    return pl.pallas_call(
        _s_mem_kernel,
        grid=(n_seq // nb,),
        in_specs=[tok, mem, mem],
        out_specs=tok,
        out_shape=jax.ShapeDtypeStruct(q_pad.shape, F32),
        compiler_params=_params(1),
        name="s_mem",
    )(q_pad, mem_k, mem_v)


def _page_copies(pt_ref, ck_hbm, cv_hbm, kbuf, vbuf, sem, seq, slot, n_pages):
    copies = []
    for p in range(n_pages):
        page = pt_ref[seq * n_pages + p]
        copies.append(pltpu.make_async_copy(ck_hbm.at[page], kbuf.at[slot, :, p], sem.at[0, slot]))
        copies.append(pltpu.make_async_copy(cv_hbm.at[page], vbuf.at[slot, :, p], sem.at[1, slot]))
    return copies


def _s_paged_kernel(pt_ref, q_ref, kn_ref, vn_ref, lam_ref, sg_ref, ck_hbm, cv_hbm, o_ref,
                    kbuf, vbuf, sem, *, n_pages, n_new):
    seq = pl.program_id(0)
    slot = seq % 2
    fetch = functools.partial(_page_copies, pt_ref, ck_hbm, cv_hbm, kbuf, vbuf, sem, n_pages=n_pages)

    @pl.when(seq == 0)
    def _():
        for c in fetch(seq=0, slot=0):
            c.start()

    @pl.when(seq + 1 < pl.num_programs(0))
    def _():
        for c in fetch(seq=seq + 1, slot=1 - slot):
            c.start()

    for c in fetch(seq=seq, slot=slot):
        c.wait()
    lam = _lambda(lam_ref[...])
    first_map = lax.broadcasted_iota(jnp.int32, (PAD_T, DIFF_VD), 1) < DIFF_HD
    t_q = jnp.bitwise_and(lax.broadcasted_iota(jnp.int32, (2 * PAD_T, 1), 0), PAD_T - 1)
    n_past = n_pages * PAGE_SIZE

    outs = []
    for h in range(DIFF_HEADS):
        lanes = slice(h * DIFF_VD, (h + 1) * DIFF_VD)
        q = q_ref[0, :, lanes]
        qf = jnp.concatenate([jnp.where(first_map, q, 0.0), jnp.where(first_map, 0.0, q)], axis=0)
        k_past = kbuf[slot, h].reshape(n_past, DIFF_VD).astype(BF16)
        v_past = vbuf[slot, h].reshape(n_past, DIFF_VD).astype(BF16)

        s_past = _dot_nt(qf.astype(BF16), k_past)
        s_new = [jnp.where(t <= t_q,
                           jnp.sum(qf * kn_ref[0, t:t + 1, lanes], axis=-1, keepdims=True), NEG_INF)
                 for t in range(n_new)]
        m = jnp.max(s_past, axis=-1, keepdims=True)
        for s in s_new:
            m = jnp.maximum(m, s)
        p_past = jnp.exp(s_past - m)
        p_new = [jnp.exp(s - m) for s in s_new]
        l = jnp.sum(p_past, axis=-1, keepdims=True)
        for p in p_new:
            l = l + p
        w1 = 1.0 / l[:PAD_T]
        w2 = lam / l[PAD_T:]

        acc = _dot((p_past[:PAD_T] * w1 - p_past[PAD_T:] * w2).astype(BF16), v_past)
        for t in range(n_new):
            acc = acc + (p_new[t][:PAD_T] * w1 - p_new[t][PAD_T:] * w2) * vn_ref[0, t:t + 1, lanes]
        outs.append(_rms(acc, sg_ref[:, lanes]) * (1.0 - LAM_INIT_B))
    o_ref[0] = jnp.concatenate(outs, axis=-1)


def _s_paged(page_table, q_pad, k_new, v_new, lam, sg, cache_k, cache_v):
    n_seq, n_pages = page_table.shape
    n_new = k_new.shape[1]
    tok = lambda r: pl.BlockSpec((1, r, DIFF_W), lambda b, pt: (b, 0, 0))
    hbm = pl.BlockSpec(memory_space=pl.ANY)
    page_buf = pltpu.VMEM((2, DIFF_HEADS, n_pages, PAGE_SIZE, DIFF_VD), F32)
    grid_spec = pltpu.PrefetchScalarGridSpec(
        num_scalar_prefetch=1,
        grid=(n_seq,),
        in_specs=[tok(PAD_T), tok(n_new), tok(n_new), _const_spec(lam.shape), _const_spec(sg.shape),
                  hbm, hbm],
        out_specs=tok(PAD_T),
        scratch_shapes=[page_buf, page_buf, pltpu.SemaphoreType.DMA((2, 2))],
    )
    return pl.pallas_call(
        functools.partial(_s_paged_kernel, n_pages=n_pages, n_new=n_new),
        grid_spec=grid_spec,
        out_shape=jax.ShapeDtypeStruct(q_pad.shape, F32),
        compiler_params=_params(1),
        name="s_paged",
    )(page_table.reshape(-1), q_pad, k_new, v_new, lam, sg, cache_k, cache_v)


def _to_time_major(a):
    return jnp.swapaxes(a, 0, 1)


def _pad_tokens(a_tm):
    a = jnp.swapaxes(a_tm, 0, 1)
    return jnp.pad(a, ((0, 0), (0, PAD_T - a.shape[1]), (0, 0)))


def _from_padded(a, n_t, dtype):
    return jnp.swapaxes(a[:, :n_t], 0, 1).astype(dtype)


def kernel(x_prompt, x_sample, mem_prompt, cache_mem_k, cache_mem_v, state_conv_a, state_ffn_conv, cache_k, cache_v, page_table, norm_mix, norm_ffn, norm_mem, w_mem_kv, mem_q_norm, mem_k_norm, w_in_a, conv_a_w, conv_a_b, ln_a_g, ln_a_b, w_out_a, norm_kv, w_kv, k_norm, w_in_b, q_norm, lambdas, subln_g, w_out_b, w_up, ffn_conv_w, ffn_conv_b, w_down):
    batch, seq, _ = x_prompt.shape
    n_seq, n_t, _ = x_sample.shape
    depth = norm_mix.shape[0]
    assert depth == 2 and w_in_a.shape[0] == 1 and w_in_b.shape[0] == 1
    assert seq % ROW_TILE == 0 and n_t <= PAD_T and n_seq % 32 == 0

    row = lambda a: a.reshape(1, -1)
    gmix = [row(norm_mix[l]) for l in range(depth)]
    gffn = [row(norm_ffn[l]) for l in range(depth)]
    mqg = [row(jnp.tile(mem_q_norm[l], MEM_HEADS)) for l in range(depth)]
    mkg = jnp.tile(mem_k_norm, (1, MEM_HEADS)).reshape(depth, 1, MEM_W)
    kg = row(jnp.tile(k_norm.reshape(-1), DIFF_HEADS))
    qg = row(jnp.tile(q_norm[0].reshape(-1), DIFF_HEADS))
    sg = row(jnp.tile(subln_g[0], DIFF_HEADS))
    lam = lambdas[0]
    win_a = w_in_a[0].astype(BF16)
    wout_a = w_out_a[0].astype(BF16)
    wkv = w_kv.astype(BF16)
    win_b = w_in_b[0].astype(BF16)
    wout_b = w_out_b[0].astype(BF16)
    wup = [w_up[l].astype(BF16) for l in range(depth)]
    wdn = [w_down[l].astype(BF16) for l in range(depth)]
    cw_a, cb_a = conv_a_w[0], row(conv_a_b[0])
    lng, lnb = row(ln_a_g[0]), row(ln_a_b[0])
    fcw = [ffn_conv_w[l] for l in range(depth)]
    fcb = [row(ffn_conv_b[l]) for l in range(depth)]

    p_mem_k, p_mem_v = _mem_kv(mem_prompt, norm_mem.reshape(depth, 1, D_MODEL), w_mem_kv.astype(BF16), mkg)
    x1, conv_tail = _p_mixa(x_prompt, gmix[0], win_a, cw_a, cb_a, lng, lnb, mqg[0], p_mem_k, p_mem_v, wout_a)
    x2, p_ffn0 = _p_ffn(x1, gffn[0], wup[0], fcw[0], fcb[0], wdn[0])
    p_k, p_v, kb, vb, qb, mb = _p_mixb(x2, row(norm_kv), wkv, kg, gmix[1], win_b, qg, mqg[1], p_mem_k, p_mem_v)
    x3 = _p_att(qb, kb, vb, mb, x2, lam, sg, wout_b)
    y_prompt, p_ffn1 = _p_ffn(x3, gffn[1], wup[1], fcw[1], fcb[1], wdn[1])

    hd = (MEM_HEADS, MEM_HD)
    p_mem_k = p_mem_k.reshape(depth, batch, MEM_LEN, *hd)
    p_mem_v = p_mem_v.reshape(depth, batch, MEM_LEN, *hd)
    p_conv_a = conv_tail[None, :, CARRY_ROWS - (CONV_A_WIDTH - 1):, :]
    p_ffn_conv = jnp.stack([p_ffn0, p_ffn1])
    p_k = p_k.reshape(batch, seq, DIFF_HEADS, DIFF_VD)
    p_v = p_v.reshape(batch, seq, DIFF_HEADS, DIFF_VD)

    smem_k = jnp.transpose(cache_mem_k, (0, 1, 3, 4, 2)).reshape(depth, n_seq, MEM_W, MEM_LEN)
    smem_v = jnp.transpose(cache_mem_v, (0, 1, 3, 4, 2)).reshape(depth, n_seq, MEM_W, MEM_LEN)
    xs = _to_time_major(x_sample)
    glu, c_a, mq_a = _s_mixa(xs, _to_time_major(state_conv_a[0]), gmix[0], win_a, cw_a, cb_a, lng, lnb, mqg[0])
    m_a = _from_padded(_s_mem(_pad_tokens(mq_a), smem_k, smem_v, 0), n_t, BF16)
    flat = lambda a: a.reshape(n_t * n_seq, a.shape[-1])
    xs2, s_ffn0 = _s_ffn(flat(xs), flat(c_a), flat(m_a), wout_a, gffn[0], wup[0], fcw[0], fcb[0], wdn[0],
                         _to_time_major(state_ffn_conv[0]))
    k_new, v_new, q_s, mq_b = _s_mixb(xs2, row(norm_kv), wkv, kg, gmix[1], win_b, qg, mqg[1])
    unflat = lambda a: a.reshape(n_t, n_seq, a.shape[-1])
    m_b = _from_padded(_s_mem(_pad_tokens(unflat(mq_b)), smem_k, smem_v, 1), n_t, BF16)
    k_new_sm = jnp.swapaxes(unflat(k_new), 0, 1)
    v_new_sm = jnp.swapaxes(unflat(v_new), 0, 1)
    c_b = _s_paged(page_table, _pad_tokens(unflat(q_s)), k_new_sm, v_new_sm, lam, sg,
                   jnp.swapaxes(cache_k, 1, 2), jnp.swapaxes(cache_v, 1, 2))
    c_b = _from_padded(c_b, n_t, BF16)
    ys, s_ffn1 = _s_ffn(xs2, flat(c_b), flat(m_b), wout_b, gffn[1], wup[1], fcw[1], fcb[1], wdn[1],
                        _to_time_major(state_ffn_conv[1]))

    y_sample = jnp.swapaxes(unflat(ys), 0, 1)
    s_conv_a = jnp.concatenate([state_conv_a[0][:, n_t:], jnp.swapaxes(glu, 0, 1)], axis=1)[None]
    s_ffn_conv = jnp.stack([_to_time_major(s_ffn0), _to_time_major(s_ffn1)])
    s_k = k_new_sm.reshape(n_seq, n_t, DIFF_HEADS, DIFF_VD)
    s_v = v_new_sm.reshape(n_seq, n_t, DIFF_HEADS, DIFF_VD)
    return (y_prompt, y_sample, p_mem_k, p_mem_v, p_conv_a, p_ffn_conv, p_k, p_v,
            s_conv_a, s_ffn_conv, s_k, s_v)
```

```python
import functools
import math

import jax
import jax.numpy as jnp
from jax import lax
from jax.experimental import pallas as pl
from jax.experimental.pallas import tpu as pltpu

F32 = jnp.float32
BF16 = jnp.bfloat16

D_MODEL = 1024
MEM_LEN = 256
MEM_HEADS = 4
MEM_W = 256
MEM_HD = 64
C_A = 768
CONV_A_WIDTH = 31
DIFF_HD = 64
DIFF_VD = 128
DIFF_HEADS = 6
DIFF_W = 768
D_FF = 2816
PAGE_SIZE = 128
EPS = 1e-6
LN_EPS = 1e-5
NEG_INF = -1e30
LAM_INIT_B = 0.8 - 0.6 * math.exp(-0.3 * 1)

FF_CHUNK = 256
N_FF_CHUNKS = D_FF // FF_CHUNK
ROW_TILE = 512
CONV_ROWS = 128
CONV_LANES = 256
CARRY_ROWS = 32
ATT_HEADS_PER_TRIP = 3
PAD_T = 8
VMEM_LIMIT = 56 * 1024 * 1024


def _const_spec(shape):
    n = len(shape)
    return pl.BlockSpec(shape, lambda *_: (0,) * n, pipeline_mode=pl.Buffered(1))


def _params(n_grid):
    return pltpu.CompilerParams(dimension_semantics=("arbitrary",) * n_grid,
                                vmem_limit_bytes=VMEM_LIMIT)


def _dot(a, b):
    return jnp.dot(a, b, preferred_element_type=F32)


def _dot_nt(a, b):
    return lax.dot_general(a, b, (((1,), (1,)), ((), ())), preferred_element_type=F32)


def _rms(x, g):
    return x * lax.rsqrt(jnp.mean(x * x, axis=-1, keepdims=True) + EPS) * g


def _group_rms(x, g, group):
    width = x.shape[-1]
    blk = 256
    r = lax.broadcasted_iota(jnp.int32, (blk, blk), 0)
    c = lax.broadcasted_iota(jnp.int32, (blk, blk), 1)
    shift = group.bit_length() - 1
    ones = (jnp.right_shift(r, shift) == jnp.right_shift(c, shift)).astype(BF16)
    outs = []
    for i in range(width // blk):
        xc = x[:, i * blk:(i + 1) * blk]
        ss = _dot((xc * xc).astype(BF16), ones)
        outs.append(xc * lax.rsqrt(ss * (1.0 / group) + EPS) * g[:, i * blk:(i + 1) * blk])
    return outs[0] if len(outs) == 1 else jnp.concatenate(outs, axis=-1)


def _layernorm(x, g, b):
    xc = x - jnp.mean(x, axis=-1, keepdims=True)
    var = jnp.mean(xc * xc, axis=-1, keepdims=True)
    return xc * lax.rsqrt(var + LN_EPS) * g + b


def _silu(x):
    return x * jax.nn.sigmoid(x)


def _lane_group(shape, shift):
    return jnp.right_shift(lax.broadcasted_iota(jnp.int32, shape, len(shape) - 1), shift)


def _mem_attend_shared(mq, mk, mv):
    head_of_lane = _lane_group(mq.shape, 6)
    out = jnp.zeros(mq.shape, F32)
    for h in range(MEM_HEADS):
        sel = head_of_lane == h
        s = _dot_nt(jnp.where(sel, mq, 0.0).astype(BF16), mk)
        p = jnp.exp(s - jnp.max(s, axis=-1, keepdims=True))
        o = _dot(p.astype(BF16), mv) / jnp.sum(p, axis=-1, keepdims=True)
        out = jnp.where(sel, o, out)
    return out


def _lambda(lf):
    s1 = jnp.sum(lf[0:1] * lf[1:2], axis=-1, keepdims=True)
    s2 = jnp.sum(lf[2:3] * lf[3:4], axis=-1, keepdims=True)
    return jnp.exp(s1) - jnp.exp(s2) + LAM_INIT_B


def _mem_kv_kernel(mem_ref, g_ref, w_ref, kg_ref, k_ref, v_ref):
    h = _rms(mem_ref[0], g_ref[0]).astype(BF16)
    kv = _dot(h, w_ref[0])
    k_ref[0, 0] = _group_rms(kv[:, :MEM_W], kg_ref[0], MEM_HD)
    v_ref[0, 0] = kv[:, MEM_W:]


def _mem_kv(mem, norm_mem, w_mem_kv, mem_k_gain):
    depth, batch = w_mem_kv.shape[0], mem.shape[0]
    out = jax.ShapeDtypeStruct((depth, batch, MEM_LEN, MEM_W), F32)
    return pl.pallas_call(
        _mem_kv_kernel,
        grid=(depth, batch),
        in_specs=[
            pl.BlockSpec((1, MEM_LEN, D_MODEL), lambda l, b: (b, 0, 0)),
            pl.BlockSpec((1, 1, D_MODEL), lambda l, b: (l, 0, 0)),
            pl.BlockSpec((1, D_MODEL, 2 * MEM_W), lambda l, b: (l, 0, 0)),
            pl.BlockSpec((1, 1, MEM_W), lambda l, b: (l, 0, 0)),
        ],
        out_specs=[pl.BlockSpec((1, 1, MEM_LEN, MEM_W), lambda l, b: (l, b, 0, 0))] * 2,
        out_shape=[out, out],
        compiler_params=_params(2),
        name="mem_kv",
    )(mem, norm_mem, w_mem_kv, mem_k_gain)


def _conv31_tile(ext_ref, cw_ref, cb_ref, r0, lanes):
    span = CONV_ROWS + CARRY_ROWS + 8
    e = ext_ref[r0:r0 + span, lanes]
    acc = jnp.broadcast_to(cb_ref[:, lanes], (CONV_ROWS, CONV_LANES))
    for r in range(8):
        u = e if r == 0 else pltpu.roll(e, span - r, 0)
        for o in range(2, CONV_A_WIDTH + 2):
            if o % 8 == r:
                a = o // 8
                acc = acc + cw_ref[o - 2:o - 1, lanes] * u[8 * a:8 * a + CONV_ROWS]
    return acc


def _p_mixa_kernel(x_ref, gmix_ref, win_ref, cw_ref, cb_ref, lng_ref, lnb_ref, mqg_ref, mk_ref, mv_ref,
                   wout_ref, x1_ref, st_ref, ext_ref, cm_ref):
    tile = x_ref.shape[1]

    @pl.when(pl.program_id(1) == 0)
    def _():
        ext_ref[0:CARRY_ROWS, :] = jnp.zeros((CARRY_ROWS, C_A), F32)
        ext_ref[CARRY_ROWS + tile:CARRY_ROWS + tile + 8, :] = jnp.zeros((8, C_A), F32)

    x = x_ref[0]
    h = _rms(x, gmix_ref[...]).astype(BF16)
    z = _dot(h, win_ref[...])
    ext_ref[CARRY_ROWS:CARRY_ROWS + tile, :] = z[:, :C_A] * jax.nn.sigmoid(z[:, C_A:2 * C_A])

    for r0 in range(0, tile, CONV_ROWS):
        y = jnp.concatenate(
            [_conv31_tile(ext_ref, cw_ref, cb_ref, r0, slice(l0, l0 + CONV_LANES))
             for l0 in range(0, C_A, CONV_LANES)], axis=-1)
        c = _silu(_layernorm(y, lng_ref[...], lnb_ref[...]))
        cm_ref[r0:r0 + CONV_ROWS, 0:C_A] = c.astype(BF16)

    st_ref[0] = ext_ref[tile:tile + CARRY_ROWS, :]
    ext_ref[0:CARRY_ROWS, :] = ext_ref[tile:tile + CARRY_ROWS, :]

    mq = _group_rms(z[:, 2 * C_A:], mqg_ref[...], MEM_HD) * (MEM_HD ** -0.5)
    m = _mem_attend_shared(mq, mk_ref[0, 0].astype(BF16), mv_ref[0, 0].astype(BF16))
    cm_ref[:, C_A:] = m.astype(BF16)
    x1_ref[0] = x + _dot(cm_ref[...], wout_ref[...])


def _p_mixa(x, gmix, win, cw, cb, lng, lnb, mqg, mem_k, mem_v, wout):
    batch, seq, _ = x.shape
    tile = ROW_TILE
    tok = pl.BlockSpec((1, tile, D_MODEL), lambda b, t: (b, t, 0))
    mem = pl.BlockSpec((1, 1, MEM_LEN, MEM_W), lambda b, t: (0, b, 0, 0))
    return pl.pallas_call(
        _p_mixa_kernel,
        grid=(batch, seq // tile),
        in_specs=[tok, _const_spec(gmix.shape), _const_spec(win.shape), _const_spec(cw.shape),
                  _const_spec(cb.shape), _const_spec(lng.shape), _const_spec(lnb.shape),
                  _const_spec(mqg.shape), mem, mem, _const_spec(wout.shape)],
        out_specs=[tok, pl.BlockSpec((1, CARRY_ROWS, C_A), lambda b, t: (b, 0, 0))],
        out_shape=[jax.ShapeDtypeStruct(x.shape, F32),
                   jax.ShapeDtypeStruct((batch, CARRY_ROWS, C_A), F32)],
        scratch_shapes=[pltpu.VMEM((CARRY_ROWS + tile + 8, C_A), F32),
                        pltpu.VMEM((tile, D_MODEL), BF16)],
        compiler_params=_params(2),
        name="p_mixa",
    )(x, gmix, win, cw, cb, lng, lnb, mqg, mem_k, mem_v, wout)


def _ffn_hidden(h, wup_ref, cw_ref, cb_ref, conv_fn, act_ref):
    for j in range(N_FF_CHUNKS):
        lanes = slice(j * FF_CHUNK, (j + 1) * FF_CHUNK)
        gate = _dot(h, wup_ref[:, lanes])
        val = _dot(h, wup_ref[:, D_FF + j * FF_CHUNK:D_FF + (j + 1) * FF_CHUNK])
        g2, g1 = conv_fn(lanes, gate)
        conv = (cw_ref[0:1, lanes] * g2 + cw_ref[1:2, lanes] * g1 + cw_ref[2:3, lanes] * gate
                + cb_ref[:, lanes])
        act_ref[:, lanes] = (jax.nn.gelu(conv) * val).astype(BF16)


def _p_ffn_kernel(x_ref, g_ref, wup_ref, cw_ref, cb_ref, wdn_ref, o_ref, st_ref, carry_ref, act_ref):
    tile = x_ref.shape[1]

    @pl.when(pl.program_id(1) == 0)
    def _():
        carry_ref[...] = jnp.zeros(carry_ref.shape, F32)

    x = x_ref[0]
    h = _rms(x, g_ref[...]).astype(BF16)
    row = lax.broadcasted_iota(jnp.int32, (tile, FF_CHUNK), 0)

    def shifted(lanes, gate):
        c2 = carry_ref[0:1, lanes]
        c1 = carry_ref[1:2, lanes]
        g1 = jnp.where(row == 0, c1, pltpu.roll(gate, 1, 0))
        g2 = jnp.where(row == 0, c2, jnp.where(row == 1, c1, pltpu.roll(gate, 2, 0)))
        carry_ref[0:2, lanes] = gate[tile - 2:tile]
        return g2, g1

    _ffn_hidden(h, wup_ref, cw_ref, cb_ref, shifted, act_ref)
    o_ref[0] = x + _dot(act_ref[...], wdn_ref[...])
    st_ref[0] = carry_ref[0:2, :]


def _p_ffn(x, g, wup, cw, cb, wdn):
    batch, seq, _ = x.shape
    tile = ROW_TILE
    tok = pl.BlockSpec((1, tile, D_MODEL), lambda b, t: (b, t, 0))
    return pl.pallas_call(
        _p_ffn_kernel,
        grid=(batch, seq // tile),
        in_specs=[tok, _const_spec(g.shape), _const_spec(wup.shape), _const_spec(cw.shape),
                  _const_spec(cb.shape), _const_spec(wdn.shape)],
        out_specs=[tok, pl.BlockSpec((1, 2, D_FF), lambda b, t: (b, 0, 0))],
        out_shape=[jax.ShapeDtypeStruct(x.shape, F32), jax.ShapeDtypeStruct((batch, 2, D_FF), F32)],
        scratch_shapes=[pltpu.VMEM((8, D_FF), F32), pltpu.VMEM((tile, D_FF), BF16)],
        compiler_params=_params(2),
        name="p_ffn",
    )(x, g, wup, cw, cb, wdn)


def _s_ffn_kernel(x_ref, c_ref, m_ref, wout_ref, g_ref, wup_ref, cw_ref, cb_ref, wdn_ref, st_ref,
                  o_ref, nst_ref, act_ref):
    n_seq = st_ref.shape[1]
    o_ref[...] = (x_ref[...] + _dot(c_ref[...], wout_ref[0:DIFF_W, :])
                  + _dot(m_ref[...], wout_ref[DIFF_W:, :]))
    h = _rms(o_ref[...], g_ref[...]).astype(BF16)

    def shifted(lanes, gate):
        ext = [st_ref[0, :, lanes], st_ref[1, :, lanes]] + [
            gate[t * n_seq:(t + 1) * n_seq] for t in range(gate.shape[0] // n_seq)]
        nst_ref[0, :, lanes] = ext[-2]
        nst_ref[1, :, lanes] = ext[-1]
        return jnp.concatenate(ext[:-2], axis=0), jnp.concatenate(ext[1:-1], axis=0)

    _ffn_hidden(h, wup_ref, cw_ref, cb_ref, shifted, act_ref)
    o_ref[...] = o_ref[...] + _dot(act_ref[...], wdn_ref[...])


def _s_ffn(x, c, m, wout, g, wup, cw, cb, wdn, st):
    args = (x, c, m, wout, g, wup, cw, cb, wdn, st)
    return pl.pallas_call(
        _s_ffn_kernel,
        grid=(1,),
        in_specs=[_const_spec(a.shape) for a in args],
        out_specs=[_const_spec(x.shape), _const_spec(st.shape)],
        out_shape=[jax.ShapeDtypeStruct(x.shape, F32), jax.ShapeDtypeStruct(st.shape, F32)],
        scratch_shapes=[pltpu.VMEM((x.shape[0], D_FF), BF16)],
        compiler_params=_params(1),
        name="s_ffn",
    )(*args)


def _mixb_project(x, gkv, wkv, kg, gmix, winb, qg, mqg):
    xn = x * lax.rsqrt(jnp.mean(x * x, axis=-1, keepdims=True) + EPS)
    kv = _dot((xn * gkv).astype(BF16), wkv)
    qm = _dot((xn * gmix).astype(BF16), winb)
    k = _group_rms(kv[:, :DIFF_W], kg, DIFF_HD)
    v = kv[:, DIFF_W:]
    q = _group_rms(qm[:, :DIFF_W], qg, DIFF_HD) * (DIFF_HD ** -0.5)
    mq = _group_rms(qm[:, DIFF_W:], mqg, MEM_HD) * (MEM_HD ** -0.5)
    return k, v, q, mq


def _p_mixb_kernel(x_ref, gkv_ref, wkv_ref, kg_ref, gmix_ref, winb_ref, qg_ref, mqg_ref, mk_ref, mv_ref,
                   k_ref, v_ref, kb_ref, vb_ref, qb_ref, mb_ref):
    k, v, q, mq = _mixb_project(x_ref[0], gkv_ref[...], wkv_ref[...], kg_ref[...], gmix_ref[...],
                                winb_ref[...], qg_ref[...], mqg_ref[...])
    k_ref[0] = k
    v_ref[0] = v
    kb_ref[0] = k.astype(BF16)
    vb_ref[0] = v.astype(BF16)
    qb_ref[0] = q.astype(BF16)
    m = _mem_attend_shared(mq, mk_ref[0, 0].astype(BF16), mv_ref[0, 0].astype(BF16))
    mb_ref[0] = m.astype(BF16)


def _p_mixb(x, gkv, wkv, kg, gmix, winb, qg, mqg, mem_k, mem_v):
    batch, seq, _ = x.shape
    tile = ROW_TILE
    tok = lambda w: pl.BlockSpec((1, tile, w), lambda b, t: (b, t, 0))
    mem = pl.BlockSpec((1, 1, MEM_LEN, MEM_W), lambda b, t: (1, b, 0, 0))
    shp = lambda w, dt: jax.ShapeDtypeStruct((batch, seq, w), dt)
    return pl.pallas_call(
        _p_mixb_kernel,
        grid=(batch, seq // tile),
        in_specs=[tok(D_MODEL)] + [_const_spec(a.shape) for a in (gkv, wkv, kg, gmix, winb, qg, mqg)]
        + [mem, mem],
        out_specs=[tok(DIFF_W)] * 5 + [tok(MEM_W)],
        out_shape=[shp(DIFF_W, F32), shp(DIFF_W, F32), shp(DIFF_W, BF16), shp(DIFF_W, BF16),
                   shp(DIFF_W, BF16), shp(MEM_W, BF16)],
        compiler_params=_params(2),
        name="p_mixb",
    )(x, gkv, wkv, kg, gmix, winb, qg, mqg, mem_k, mem_v)


def _s_mixb_kernel(x_ref, gkv_ref, wkv_ref, kg_ref, gmix_ref, winb_ref, qg_ref, mqg_ref,
                   k_ref, v_ref, q_ref, mq_ref):
    k, v, q, mq = _mixb_project(x_ref[...], gkv_ref[...], wkv_ref[...], kg_ref[...], gmix_ref[...],
                                winb_ref[...], qg_ref[...], mqg_ref[...])
    k_ref[...] = k
    v_ref[...] = v
    q_ref[...] = q
    mq_ref[...] = mq


def _s_mixb(x, gkv, wkv, kg, gmix, winb, qg, mqg):
    n = x.shape[0]
    args = (x, gkv, wkv, kg, gmix, winb, qg, mqg)
    widths = (DIFF_W, DIFF_W, DIFF_W, MEM_W)
    return pl.pallas_call(
        _s_mixb_kernel,
        grid=(1,),
        in_specs=[_const_spec(a.shape) for a in args],
        out_specs=[_const_spec((n, w)) for w in widths],
        out_shape=[jax.ShapeDtypeStruct((n, w), F32) for w in widths],
        compiler_params=_params(1),
        name="s_mixb",
    )(*args)


def _p_att_kernel(q_ref, k_ref, v_ref, m_ref, x_ref, lam_ref, sg_ref, wout_ref, o_ref, c_ref):
    tq = q_ref.shape[1]
    qi = pl.program_id(1)
    lam = _lambda(lam_ref[...])
    row = lax.broadcasted_iota(jnp.int32, (tq, tq), 0)
    col = lax.broadcasted_iota(jnp.int32, (tq, tq), 1)
    causal = col <= row
    first_map = lax.broadcasted_iota(jnp.int32, (tq, DIFF_VD), 1) < DIFF_HD

    for h0 in range(0, DIFF_HEADS, ATT_HEADS_PER_TRIP):
        lanes = [slice(h * DIFF_VD, (h + 1) * DIFF_VD) for h in range(h0, h0 + ATT_HEADS_PER_TRIP)]
        qs = []
        for ln in lanes:
            q = q_ref[0, :, ln]
            zero = jnp.zeros_like(q)
            qs.append((jnp.where(first_map, q, zero), jnp.where(first_map, zero, q)))

        def block(j, carry, masked):
            start = pl.multiple_of(j * tq, tq)
            scores = []
            for i, ln in enumerate(lanes):
                kb = k_ref[0, pl.ds(start, tq), ln]
                scores += [_dot_nt(qs[i][c], kb) for c in range(2)]
            soft = []
            for n, s in enumerate(scores):
                m_old, l_old, _ = carry[n]
                if masked:
                    s = jnp.where(causal, s, NEG_INF)
                m_new = jnp.maximum(m_old, jnp.max(s, axis=-1, keepdims=True))
                alpha = jnp.exp(m_old - m_new)
                p = jnp.exp(s - m_new)
                soft.append((m_new, alpha * l_old + jnp.sum(p, axis=-1, keepdims=True), alpha,
                             p.astype(BF16)))
            new = []
            for n, (m_new, l_new, alpha, p) in enumerate(soft):
                vb = v_ref[0, pl.ds(start, tq), lanes[n // 2]]
                new.append((m_new, l_new, alpha * carry[n][2] + _dot(p, vb)))
            return tuple(new)

        init = tuple((jnp.full((tq, 1), NEG_INF, F32), jnp.zeros((tq, 1), F32),
                      jnp.zeros((tq, DIFF_VD), F32)) for _ in range(2 * ATT_HEADS_PER_TRIP))
        carry = lax.fori_loop(0, qi, lambda j, c: block(j, c, False), init)
        carry = block(qi, carry, True)
        for i, ln in enumerate(lanes):
            (_, l1, a1), (_, l2, a2) = carry[2 * i], carry[2 * i + 1]
            o = a1 / l1 - lam * (a2 / l2)
            o = _rms(o, sg_ref[:, ln]) * (1.0 - LAM_INIT_B)
            c_ref[:, ln] = o.astype(BF16)

    o_ref[0] = (x_ref[0] + _dot(c_ref[...], wout_ref[0:DIFF_W, :])
                + _dot(m_ref[0], wout_ref[DIFF_W:, :]))


def _p_att(q, k, v, m, x, lam, sg, wout):
    batch, seq, _ = x.shape
    tile = ROW_TILE
    tok = lambda w: pl.BlockSpec((1, tile, w), lambda b, t: (b, t, 0))
    kv = pl.BlockSpec((1, seq, DIFF_W), lambda b, t: (b, 0, 0), pipeline_mode=pl.Buffered(1))
    return pl.pallas_call(
        _p_att_kernel,
        grid=(batch, seq // tile),
        in_specs=[tok(DIFF_W), kv, kv, tok(MEM_W), tok(D_MODEL), _const_spec(lam.shape),
                  _const_spec(sg.shape), _const_spec(wout.shape)],
        out_specs=tok(D_MODEL),
        out_shape=jax.ShapeDtypeStruct(x.shape, F32),
        scratch_shapes=[pltpu.VMEM((tile, DIFF_W), BF16)],
        compiler_params=_params(2),
        name="p_att",
    )(q, k, v, m, x, lam, sg, wout)


def _s_mixa_kernel(x_ref, st_ref, gmix_ref, win_ref, cw_ref, cb_ref, lng_ref, lnb_ref, mqg_ref,
                   glu_ref, c_ref, mq_ref):
    n_t, nb, _ = x_ref.shape
    n_hist = st_ref.shape[0]
    x = x_ref[...].reshape(n_t * nb, D_MODEL)
    h = _rms(x, gmix_ref[...]).astype(BF16)
    z = _dot(h, win_ref[...])
    glu = z[:, :C_A] * jax.nn.sigmoid(z[:, C_A:2 * C_A])
    glu_ref[...] = glu.reshape(n_t, nb, C_A)
    mq = _group_rms(z[:, 2 * C_A:], mqg_ref[...], MEM_HD) * (MEM_HD ** -0.5)
    mq_ref[...] = mq.reshape(n_t, nb, MEM_W)

    ys = [[] for _ in range(n_t)]
    for l0 in range(0, C_A, CONV_LANES):
        lanes = slice(l0, l0 + CONV_LANES)
        acc = [jnp.broadcast_to(cb_ref[:, lanes], (nb, CONV_LANES)) for _ in range(n_t)]
        for r in range(n_hist + n_t):
            e = st_ref[r, :, lanes] if r < n_hist else glu[(r - n_hist) * nb:(r - n_hist + 1) * nb, lanes]
            for t in range(n_t):
                if 0 <= r - t < CONV_A_WIDTH:
                    acc[t] = acc[t] + cw_ref[r - t:r - t + 1, lanes] * e
        for t in range(n_t):
            ys[t].append(acc[t])
    for t in range(n_t):
        y = jnp.concatenate(ys[t], axis=-1)
        c_ref[t] = _silu(_layernorm(y, lng_ref[...], lnb_ref[...])).astype(BF16)


def _s_mixa(x_tm, st_tm, gmix, win, cw, cb, lng, lnb, mqg):
    n_t, n_seq, _ = x_tm.shape
    nb = 32
    blk = lambda r, w: pl.BlockSpec((r, nb, w), lambda i: (0, i, 0))
    consts = (gmix, win, cw, cb, lng, lnb, mqg)
    return pl.pallas_call(
        _s_mixa_kernel,
        grid=(n_seq // nb,),
        in_specs=[blk(n_t, D_MODEL), blk(st_tm.shape[0], C_A)] + [_const_spec(a.shape) for a in consts],
        out_specs=[blk(n_t, C_A), blk(n_t, C_A), blk(n_t, MEM_W)],
        out_shape=[jax.ShapeDtypeStruct((n_t, n_seq, C_A), F32),
                   jax.ShapeDtypeStruct((n_t, n_seq, C_A), BF16),
                   jax.ShapeDtypeStruct((n_t, n_seq, MEM_W), F32)],
        compiler_params=_params(1),
        name="s_mixa",
    )(x_tm, st_tm, *consts)


def _s_mem_kernel(q_ref, mk_ref, mv_ref, o_ref):
    nb = q_ref.shape[0]
    rows = MEM_HEADS * PAD_T
    head_of_row = jnp.right_shift(lax.broadcasted_iota(jnp.int32, (rows, MEM_W), 0), 3)
    own = head_of_row == _lane_group((rows, MEM_W), 6)
    head_of_lane = _lane_group((PAD_T, MEM_W), 6)
    for b in range(nb):
        q = q_ref[b]
        qb = jnp.where(own, jnp.concatenate([q] * MEM_HEADS, axis=0), 0.0).astype(BF16)
        s = _dot(qb, mk_ref[0, b].astype(BF16))
        p = jnp.exp(s - jnp.max(s, axis=-1, keepdims=True))
        o = _dot_nt(p.astype(BF16), mv_ref[0, b].astype(BF16)) / jnp.sum(p, axis=-1, keepdims=True)
        out = jnp.zeros((PAD_T, MEM_W), F32)
        for h in range(MEM_HEADS):
            out = jnp.where(head_of_lane == h, o[h * PAD_T:(h + 1) * PAD_T], out)
        o_ref[b] = out


def _s_mem(q_pad, mem_k, mem_v, layer):
    n_seq = q_pad.shape[0]
    nb = 8
    tok = pl.BlockSpec((nb, PAD_T, MEM_W), lambda i: (i, 0, 0))
    mem = pl.BlockSpec((1, nb, MEM_LEN, MEM_W), lambda i: (layer, i, 0, 0))
    return pl.pallas_call(
        _s_mem_kernel,
        grid=(n_seq // nb,),
        in_specs=[tok, mem, mem],
        out_specs=tok,
        out_shape=jax.ShapeDtypeStruct(q_pad.shape, F32),
        compiler_params=_params(1),
        name="s_mem",
    )(q_pad, mem_k, mem_v)


def _page_copies(pt_ref, ck_hbm, cv_hbm, kbuf, vbuf, sem, seq, slot, n_pages):
    copies = []
    for p in range(n_pages):
        page = pt_ref[seq * n_pages + p]
        copies.append(pltpu.make_async_copy(ck_hbm.at[page], kbuf.at[slot, :, p], sem.at[0, slot]))
        copies.append(pltpu.make_async_copy(cv_hbm.at[page], vbuf.at[slot, :, p], sem.at[1, slot]))
    return copies


def _s_paged_kernel(pt_ref, q_ref, kn_ref, vn_ref, lam_ref, sg_ref, ck_hbm, cv_hbm, o_ref,
                    kbuf, vbuf, sem, *, n_pages, n_new):
    seq = pl.program_id(0)
    slot = seq % 2
    fetch = functools.partial(_page_copies, pt_ref, ck_hbm, cv_hbm, kbuf, vbuf, sem, n_pages=n_pages)

    @pl.when(seq == 0)
    def _():
        for c in fetch(seq=0, slot=0):
            c.start()

    @pl.when(seq + 1 < pl.num_programs(0))
    def _():
        for c in fetch(seq=seq + 1, slot=1 - slot):
            c.start()

    for c in fetch(seq=seq, slot=slot):
        c.wait()
    lam = _lambda(lam_ref[...])
    first_map = lax.broadcasted_iota(jnp.int32, (PAD_T, DIFF_VD), 1) < DIFF_HD
    t_q = jnp.bitwise_and(lax.broadcasted_iota(jnp.int32, (2 * PAD_T, 1), 0), PAD_T - 1)
    n_past = n_pages * PAGE_SIZE

    head_lanes = [slice(h * DIFF_VD, (h + 1) * DIFF_VD) for h in range(DIFF_HEADS)]
    qfs, s_pasts = [], []
    for h, lanes in enumerate(head_lanes):
        q = q_ref[0, :, lanes]
        qf = jnp.concatenate([jnp.where(first_map, q, 0.0), jnp.where(first_map, 0.0, q)], axis=0)
        k_past = kbuf[slot, h].reshape(n_past, DIFF_VD).astype(BF16)
        qfs.append(qf)
        s_pasts.append(_dot_nt(qf.astype(BF16), k_past))

    weights, new_terms = [], []
    for h, lanes in enumerate(head_lanes):
        s_past, qf = s_pasts[h], qfs[h]
        s_new = [jnp.where(t <= t_q,
                           jnp.sum(qf * kn_ref[0, t:t + 1, lanes], axis=-1, keepdims=True), NEG_INF)
                 for t in range(n_new)]
        m = jnp.max(s_past, axis=-1, keepdims=True)
        for s in s_new:
            m = jnp.maximum(m, s)
        p_past = jnp.exp(s_past - m)
        p_new = [jnp.exp(s - m) for s in s_new]
        l = jnp.sum(p_past, axis=-1, keepdims=True)
        for p in p_new:
            l = l + p
        w1 = 1.0 / l[:PAD_T]
        w2 = lam / l[PAD_T:]
        weights.append((p_past[:PAD_T] * w1 - p_past[PAD_T:] * w2).astype(BF16))
        new_terms.append([p_new[t][:PAD_T] * w1 - p_new[t][PAD_T:] * w2 for t in range(n_new)])

    outs = []
    for h, lanes in enumerate(head_lanes):
        acc = _dot(weights[h], vbuf[slot, h].reshape(n_past, DIFF_VD).astype(BF16))
        for t in range(n_new):
            acc = acc + new_terms[h][t] * vn_ref[0, t:t + 1, lanes]
        outs.append(_rms(acc, sg_ref[:, lanes]) * (1.0 - LAM_INIT_B))
    o_ref[0] = jnp.concatenate(outs, axis=-1)


def _s_paged(page_table, q_pad, k_new, v_new, lam, sg, cache_k, cache_v):
    n_seq, n_pages = page_table.shape
    n_new = k_new.shape[1]
    tok = lambda r: pl.BlockSpec((1, r, DIFF_W), lambda b, pt: (b, 0, 0))
    hbm = pl.BlockSpec(memory_space=pl.ANY)
    page_buf = pltpu.VMEM((2, DIFF_HEADS, n_pages, PAGE_SIZE, DIFF_VD), F32)
    grid_spec = pltpu.PrefetchScalarGridSpec(
        num_scalar_prefetch=1,
        grid=(n_seq,),
        in_specs=[tok(PAD_T), tok(n_new), tok(n_new), _const_spec(lam.shape), _const_spec(sg.shape),
                  hbm, hbm],
        out_specs=tok(PAD_T),
        scratch_shapes=[page_buf, page_buf, pltpu.SemaphoreType.DMA((2, 2))],
    )
    return pl.pallas_call(
        functools.partial(_s_paged_kernel, n_pages=n_pages, n_new=n_new),
        grid_spec=grid_spec,
        out_shape=jax.ShapeDtypeStruct(q_pad.shape, F32),
        compiler_params=_params(1),
        name="s_paged",
    )(page_table.reshape(-1), q_pad, k_new, v_new, lam, sg, cache_k, cache_v)


def _to_time_major(a):
    return jnp.swapaxes(a, 0, 1)


def _pad_tokens(a_tm):
    a = jnp.swapaxes(a_tm, 0, 1)
    return jnp.pad(a, ((0, 0), (0, PAD_T - a.shape[1]), (0, 0)))


def _from_padded(a, n_t, dtype):
    return jnp.swapaxes(a[:, :n_t], 0, 1).astype(dtype)


def kernel(x_prompt, x_sample, mem_prompt, cache_mem_k, cache_mem_v, state_conv_a, state_ffn_conv, cache_k, cache_v, page_table, norm_mix, norm_ffn, norm_mem, w_mem_kv, mem_q_norm, mem_k_norm, w_in_a, conv_a_w, conv_a_b, ln_a_g, ln_a_b, w_out_a, norm_kv, w_kv, k_norm, w_in_b, q_norm, lambdas, subln_g, w_out_b, w_up, ffn_conv_w, ffn_conv_b, w_down):
    batch, seq, _ = x_prompt.shape
    n_seq, n_t, _ = x_sample.shape
    depth = norm_mix.shape[0]
    assert depth == 2 and w_in_a.shape[0] == 1 and w_in_b.shape[0] == 1
    assert seq % ROW_TILE == 0 and n_t <= PAD_T and n_seq % 32 == 0

    row = lambda a: a.reshape(1, -1)
    gmix = [row(norm_mix[l]) for l in range(depth)]
    gffn = [row(norm_ffn[l]) for l in range(depth)]
    mqg = [row(jnp.tile(mem_q_norm[l], MEM_HEADS)) for l in range(depth)]
    mkg = jnp.tile(mem_k_norm, (1, MEM_HEADS)).reshape(depth, 1, MEM_W)
    kg = row(jnp.tile(k_norm.reshape(-1), DIFF_HEADS))
    qg = row(jnp.tile(q_norm[0].reshape(-1), DIFF_HEADS))
    sg = row(jnp.tile(subln_g[0], DIFF_HEADS))
    lam = lambdas[0]
    win_a = w_in_a[0].astype(BF16)
    wout_a = w_out_a[0].astype(BF16)
    wkv = w_kv.astype(BF16)
    win_b = w_in_b[0].astype(BF16)
    wout_b = w_out_b[0].astype(BF16)
    wup = [w_up[l].astype(BF16) for l in range(depth)]
    wdn = [w_down[l].astype(BF16) for l in range(depth)]
    cw_a, cb_a = conv_a_w[0], row(conv_a_b[0])
    lng, lnb = row(ln_a_g[0]), row(ln_a_b[0])
    fcw = [ffn_conv_w[l] for l in range(depth)]
    fcb = [row(ffn_conv_b[l]) for l in range(depth)]

    p_mem_k, p_mem_v = _mem_kv(mem_prompt, norm_mem.reshape(depth, 1, D_MODEL), w_mem_kv.astype(BF16), mkg)
    x1, conv_tail = _p_mixa(x_prompt, gmix[0], win_a, cw_a, cb_a, lng, lnb, mqg[0], p_mem_k, p_mem_v, wout_a)
    x2, p_ffn0 = _p_ffn(x1, gffn[0], wup[0], fcw[0], fcb[0], wdn[0])
    p_k, p_v, kb, vb, qb, mb = _p_mixb(x2, row(norm_kv), wkv, kg, gmix[1], win_b, qg, mqg[1], p_mem_k, p_mem_v)
    x3 = _p_att(qb, kb, vb, mb, x2, lam, sg, wout_b)
    y_prompt, p_ffn1 = _p_ffn(x3, gffn[1], wup[1], fcw[1], fcb[1], wdn[1])

    hd = (MEM_HEADS, MEM_HD)
    p_mem_k = p_mem_k.reshape(depth, batch, MEM_LEN, *hd)
    p_mem_v = p_mem_v.reshape(depth, batch, MEM_LEN, *hd)
    p_conv_a = conv_tail[None, :, CARRY_ROWS - (CONV_A_WIDTH - 1):, :]
    p_ffn_conv = jnp.stack([p_ffn0, p_ffn1])
    p_k = p_k.reshape(batch, seq, DIFF_HEADS, DIFF_VD)
    p_v = p_v.reshape(batch, seq, DIFF_HEADS, DIFF_VD)

    smem_k = jnp.transpose(cache_mem_k, (0, 1, 3, 4, 2)).reshape(depth, n_seq, MEM_W, MEM_LEN)
    smem_v = jnp.transpose(cache_mem_v, (0, 1, 3, 4, 2)).reshape(depth, n_seq, MEM_W, MEM_LEN)
    xs = _to_time_major(x_sample)
    glu, c_a, mq_a = _s_mixa(xs, _to_time_major(state_conv_a[0]), gmix[0], win_a, cw_a, cb_a, lng, lnb, mqg[0])
    m_a = _from_padded(_s_mem(_pad_tokens(mq_a), smem_k, smem_v, 0), n_t, BF16)
    flat = lambda a: a.reshape(n_t * n_seq, a.shape[-1])
    xs2, s_ffn0 = _s_ffn(flat(xs), flat(c_a), flat(m_a), wout_a, gffn[0], wup[0], fcw[0], fcb[0], wdn[0],
                         _to_time_major(state_ffn_conv[0]))
    k_new, v_new, q_s, mq_b = _s_mixb(xs2, row(norm_kv), wkv, kg, gmix[1], win_b, qg, mqg[1])
    unflat = lambda a: a.reshape(n_t, n_seq, a.shape[-1])
    m_b = _from_padded(_s_mem(_pad_tokens(unflat(mq_b)), smem_k, smem_v, 1), n_t, BF16)
    k_new_sm = jnp.swapaxes(unflat(k_new), 0, 1)
    v_new_sm = jnp.swapaxes(unflat(v_new), 0, 1)
    c_b = _s_paged(page_table, _pad_tokens(unflat(q_s)), k_new_sm, v_new_sm, lam, sg,
                   jnp.swapaxes(cache_k, 1, 2), jnp.swapaxes(cache_v, 1, 2))
    c_b = _from_padded(c_b, n_t, BF16)
    ys, s_ffn1 = _s_ffn(xs2, flat(c_b), flat(m_b), wout_b, gffn[1], wup[1], fcw[1], fcb[1], wdn[1],
                        _to_time_major(state_ffn_conv[1]))

    y_sample = jnp.swapaxes(unflat(ys), 0, 1)
    s_conv_a = jnp.concatenate([state_conv_a[0][:, n_t:], jnp.swapaxes(glu, 0, 1)], axis=1)[None]
    s_ffn_conv = jnp.stack([_to_time_major(s_ffn0), _to_time_major(s_ffn1)])
    s_k = k_new_sm.reshape(n_seq, n_t, DIFF_HEADS, DIFF_VD)
    s_v = v_new_sm.reshape(n_seq, n_t, DIFF_HEADS, DIFF_VD)
    return (y_prompt, y_sample, p_mem_k, p_mem_v, p_conv_a, p_ffn_conv, p_k, p_v,
            s_conv_a, s_ffn_conv, s_k, s_v)
```

```python
import functools
import math

import jax
import jax.numpy as jnp
from jax import lax
from jax.experimental import pallas as pl
from jax.experimental.pallas import tpu as pltpu

F32 = jnp.float32
BF16 = jnp.bfloat16

D_MODEL = 1024
MEM_LEN = 256
MEM_HEADS = 4
MEM_W = 256
MEM_HD = 64
C_A = 768
CONV_A_WIDTH = 31
DIFF_HD = 64
DIFF_VD = 128
DIFF_HEADS = 6
DIFF_W = 768
D_FF = 2816
PAGE_SIZE = 128
EPS = 1e-6
LN_EPS = 1e-5
NEG_INF = -1e30
LAM_INIT_B = 0.8 - 0.6 * math.exp(-0.3 * 1)
LOG2_E = math.log2(math.e)

FF_CHUNK = 256
N_FF_CHUNKS = D_FF // FF_CHUNK
ROW_TILE = 512
CONV_ROWS = 128
CONV_LANES = 256
CARRY_ROWS = 32
ATT_HEADS_PER_TRIP = 3
PAD_T = 8
VMEM_LIMIT = 56 * 1024 * 1024


def _const_spec(shape):
    n = len(shape)
    return pl.BlockSpec(shape, lambda *_: (0,) * n, pipeline_mode=pl.Buffered(1))


def _params(n_grid):
    return pltpu.CompilerParams(dimension_semantics=("arbitrary",) * n_grid,
                                vmem_limit_bytes=VMEM_LIMIT)


def _dot(a, b):
    return jnp.dot(a, b, preferred_element_type=F32)


def _dot_nt(a, b):
    return lax.dot_general(a, b, (((1,), (1,)), ((), ())), preferred_element_type=F32)


def _rms(x, g):
    return x * lax.rsqrt(jnp.mean(x * x, axis=-1, keepdims=True) + EPS) * g


def _group_rms(x, g, group):
    width = x.shape[-1]
    blk = 256
    r = lax.broadcasted_iota(jnp.int32, (blk, blk), 0)
    c = lax.broadcasted_iota(jnp.int32, (blk, blk), 1)
    shift = group.bit_length() - 1
    ones = (jnp.right_shift(r, shift) == jnp.right_shift(c, shift)).astype(BF16)
    outs = []
    for i in range(width // blk):
        xc = x[:, i * blk:(i + 1) * blk]
        ss = _dot((xc * xc).astype(BF16), ones)
        outs.append(xc * lax.rsqrt(ss * (1.0 / group) + EPS) * g[:, i * blk:(i + 1) * blk])
    return outs[0] if len(outs) == 1 else jnp.concatenate(outs, axis=-1)


def _layernorm(x, g, b):
    xc = x - jnp.mean(x, axis=-1, keepdims=True)
    var = jnp.mean(xc * xc, axis=-1, keepdims=True)
    return xc * lax.rsqrt(var + LN_EPS) * g + b


def _silu(x):
    return x * jax.nn.sigmoid(x)


def _lane_group(shape, shift):
    return jnp.right_shift(lax.broadcasted_iota(jnp.int32, shape, len(shape) - 1), shift)


def _mem_attend_shared(mq, mk, mv):
    head_of_lane = _lane_group(mq.shape, 6)
    out = jnp.zeros(mq.shape, F32)
    for h in range(MEM_HEADS):
        sel = head_of_lane == h
        s = _dot_nt(jnp.where(sel, mq, 0.0).astype(BF16), mk)
        p = jnp.exp(s - jnp.max(s, axis=-1, keepdims=True))
        o = _dot(p.astype(BF16), mv) / jnp.sum(p, axis=-1, keepdims=True)
        out = jnp.where(sel, o, out)
    return out


def _lambda(lf):
    s1 = jnp.sum(lf[0:1] * lf[1:2], axis=-1, keepdims=True)
    s2 = jnp.sum(lf[2:3] * lf[3:4], axis=-1, keepdims=True)
    return jnp.exp(s1) - jnp.exp(s2) + LAM_INIT_B


def _mem_kv_kernel(mem_ref, g_ref, w_ref, kg_ref, k_ref, v_ref):
    h = _rms(mem_ref[0], g_ref[0]).astype(BF16)
    kv = _dot(h, w_ref[0])
    k_ref[0, 0] = _group_rms(kv[:, :MEM_W], kg_ref[0], MEM_HD)
    v_ref[0, 0] = kv[:, MEM_W:]


def _mem_kv(mem, norm_mem, w_mem_kv, mem_k_gain):
    depth, batch = w_mem_kv.shape[0], mem.shape[0]
    out = jax.ShapeDtypeStruct((depth, batch, MEM_LEN, MEM_W), F32)
    return pl.pallas_call(
        _mem_kv_kernel,
        grid=(depth, batch),
        in_specs=[
            pl.BlockSpec((1, MEM_LEN, D_MODEL), lambda l, b: (b, 0, 0)),
            pl.BlockSpec((1, 1, D_MODEL), lambda l, b: (l, 0, 0)),
            pl.BlockSpec((1, D_MODEL, 2 * MEM_W), lambda l, b: (l, 0, 0)),
            pl.BlockSpec((1, 1, MEM_W), lambda l, b: (l, 0, 0)),
        ],
        out_specs=[pl.BlockSpec((1, 1, MEM_LEN, MEM_W), lambda l, b: (l, b, 0, 0))] * 2,
        out_shape=[out, out],
        compiler_params=_params(2),
        name="mem_kv",
    )(mem, norm_mem, w_mem_kv, mem_k_gain)


def _conv31_tile(ext_ref, cw_ref, cb_ref, r0, lanes):
    span = CONV_ROWS + CARRY_ROWS + 8
    e = ext_ref[r0:r0 + span, lanes]
    acc = jnp.broadcast_to(cb_ref[:, lanes], (CONV_ROWS, CONV_LANES))
    for r in range(8):
        u = e if r == 0 else pltpu.roll(e, span - r, 0)
        for o in range(2, CONV_A_WIDTH + 2):
            if o % 8 == r:
                a = o // 8
                acc = acc + cw_ref[o - 2:o - 1, lanes] * u[8 * a:8 * a + CONV_ROWS]
    return acc


def _p_mixa_kernel(x_ref, gmix_ref, win_ref, cw_ref, cb_ref, lng_ref, lnb_ref, mqg_ref, mk_ref, mv_ref,
                   wout_ref, x1_ref, st_ref, ext_ref, cm_ref):
    tile = x_ref.shape[1]

    @pl.when(pl.program_id(1) == 0)
    def _():
        ext_ref[0:CARRY_ROWS, :] = jnp.zeros((CARRY_ROWS, C_A), F32)
        ext_ref[CARRY_ROWS + tile:CARRY_ROWS + tile + 8, :] = jnp.zeros((8, C_A), F32)

    x = x_ref[0]
    h = _rms(x, gmix_ref[...]).astype(BF16)
    z = _dot(h, win_ref[...])
    ext_ref[CARRY_ROWS:CARRY_ROWS + tile, :] = z[:, :C_A] * jax.nn.sigmoid(z[:, C_A:2 * C_A])

    for r0 in range(0, tile, CONV_ROWS):
        y = jnp.concatenate(
            [_conv31_tile(ext_ref, cw_ref, cb_ref, r0, slice(l0, l0 + CONV_LANES))
             for l0 in range(0, C_A, CONV_LANES)], axis=-1)
        c = _silu(_layernorm(y, lng_ref[...], lnb_ref[...]))
        cm_ref[r0:r0 + CONV_ROWS, 0:C_A] = c.astype(BF16)

    st_ref[0] = ext_ref[tile:tile + CARRY_ROWS, :]
    ext_ref[0:CARRY_ROWS, :] = ext_ref[tile:tile + CARRY_ROWS, :]

    mq = _group_rms(z[:, 2 * C_A:], mqg_ref[...], MEM_HD) * (MEM_HD ** -0.5)
    m = _mem_attend_shared(mq, mk_ref[0, 0].astype(BF16), mv_ref[0, 0].astype(BF16))
    cm_ref[:, C_A:] = m.astype(BF16)
    x1_ref[0] = x + _dot(cm_ref[...], wout_ref[...])


def _p_mixa(x, gmix, win, cw, cb, lng, lnb, mqg, mem_k, mem_v, wout):
    batch, seq, _ = x.shape
    tile = ROW_TILE
    tok = pl.BlockSpec((1, tile, D_MODEL), lambda b, t: (b, t, 0))
    mem = pl.BlockSpec((1, 1, MEM_LEN, MEM_W), lambda b, t: (0, b, 0, 0))
    return pl.pallas_call(
        _p_mixa_kernel,
        grid=(batch, seq // tile),
        in_specs=[tok, _const_spec(gmix.shape), _const_spec(win.shape), _const_spec(cw.shape),
                  _const_spec(cb.shape), _const_spec(lng.shape), _const_spec(lnb.shape),
                  _const_spec(mqg.shape), mem, mem, _const_spec(wout.shape)],
        out_specs=[tok, pl.BlockSpec((1, CARRY_ROWS, C_A), lambda b, t: (b, 0, 0))],
        out_shape=[jax.ShapeDtypeStruct(x.shape, F32),
                   jax.ShapeDtypeStruct((batch, CARRY_ROWS, C_A), F32)],
        scratch_shapes=[pltpu.VMEM((CARRY_ROWS + tile + 8, C_A), F32),
                        pltpu.VMEM((tile, D_MODEL), BF16)],
        compiler_params=_params(2),
        name="p_mixa",
    )(x, gmix, win, cw, cb, lng, lnb, mqg, mem_k, mem_v, wout)


def _ffn_hidden(h, wup_ref, cw_ref, cb_ref, conv_fn, act_ref):
    for j in range(N_FF_CHUNKS):
        lanes = slice(j * FF_CHUNK, (j + 1) * FF_CHUNK)
        gate = _dot(h, wup_ref[:, lanes])
        val = _dot(h, wup_ref[:, D_FF + j * FF_CHUNK:D_FF + (j + 1) * FF_CHUNK])
        g2, g1 = conv_fn(lanes, gate)
        conv = (cw_ref[0:1, lanes] * g2 + cw_ref[1:2, lanes] * g1 + cw_ref[2:3, lanes] * gate
                + cb_ref[:, lanes])
        act_ref[:, lanes] = (jax.nn.gelu(conv) * val).astype(BF16)


def _p_ffn_kernel(x_ref, g_ref, wup_ref, cw_ref, cb_ref, wdn_ref, o_ref, st_ref, carry_ref, act_ref):
    tile = x_ref.shape[1]

    @pl.when(pl.program_id(1) == 0)
    def _():
        carry_ref[...] = jnp.zeros(carry_ref.shape, F32)

    x = x_ref[0]
    h = _rms(x, g_ref[...]).astype(BF16)
    row = lax.broadcasted_iota(jnp.int32, (tile, FF_CHUNK), 0)

    def shifted(lanes, gate):
        c2 = carry_ref[0:1, lanes]
        c1 = carry_ref[1:2, lanes]
        g1 = jnp.where(row == 0, c1, pltpu.roll(gate, 1, 0))
        g2 = jnp.where(row == 0, c2, jnp.where(row == 1, c1, pltpu.roll(gate, 2, 0)))
        carry_ref[0:2, lanes] = gate[tile - 2:tile]
        return g2, g1

    _ffn_hidden(h, wup_ref, cw_ref, cb_ref, shifted, act_ref)
    o_ref[0] = x + _dot(act_ref[...], wdn_ref[...])
    st_ref[0] = carry_ref[0:2, :]


def _p_ffn(x, g, wup, cw, cb, wdn):
    batch, seq, _ = x.shape
    tile = ROW_TILE
    tok = pl.BlockSpec((1, tile, D_MODEL), lambda b, t: (b, t, 0))
    return pl.pallas_call(
        _p_ffn_kernel,
        grid=(batch, seq // tile),
        in_specs=[tok, _const_spec(g.shape), _const_spec(wup.shape), _const_spec(cw.shape),
                  _const_spec(cb.shape), _const_spec(wdn.shape)],
        out_specs=[tok, pl.BlockSpec((1, 2, D_FF), lambda b, t: (b, 0, 0))],
        out_shape=[jax.ShapeDtypeStruct(x.shape, F32), jax.ShapeDtypeStruct((batch, 2, D_FF), F32)],
        scratch_shapes=[pltpu.VMEM((8, D_FF), F32), pltpu.VMEM((tile, D_FF), BF16)],
        compiler_params=_params(2),
        name="p_ffn",
    )(x, g, wup, cw, cb, wdn)


def _s_ffn_kernel(x_ref, c_ref, m_ref, wout_ref, g_ref, wup_ref, cw_ref, cb_ref, wdn_ref, st_ref,
                  o_ref, nst_ref, act_ref):
    n_seq = st_ref.shape[1]
    o_ref[...] = (x_ref[...] + _dot(c_ref[...], wout_ref[0:DIFF_W, :])
                  + _dot(m_ref[...], wout_ref[DIFF_W:, :]))
    h = _rms(o_ref[...], g_ref[...]).astype(BF16)

    def shifted(lanes, gate):
        ext = [st_ref[0, :, lanes], st_ref[1, :, lanes]] + [
            gate[t * n_seq:(t + 1) * n_seq] for t in range(gate.shape[0] // n_seq)]
        nst_ref[0, :, lanes] = ext[-2]
        nst_ref[1, :, lanes] = ext[-1]
        return jnp.concatenate(ext[:-2], axis=0), jnp.concatenate(ext[1:-1], axis=0)

    _ffn_hidden(h, wup_ref, cw_ref, cb_ref, shifted, act_ref)
    o_ref[...] = o_ref[...] + _dot(act_ref[...], wdn_ref[...])


def _s_ffn(x, c, m, wout, g, wup, cw, cb, wdn, st):
    args = (x, c, m, wout, g, wup, cw, cb, wdn, st)
    return pl.pallas_call(
        _s_ffn_kernel,
        grid=(1,),
        in_specs=[_const_spec(a.shape) for a in args],
        out_specs=[_const_spec(x.shape), _const_spec(st.shape)],
        out_shape=[jax.ShapeDtypeStruct(x.shape, F32), jax.ShapeDtypeStruct(st.shape, F32)],
        scratch_shapes=[pltpu.VMEM((x.shape[0], D_FF), BF16)],
        compiler_params=_params(1),
        name="s_ffn",
    )(*args)


def _mixb_project(x, gkv, wkv, kg, gmix, winb, qg, mqg, q_scale):
    xn = x * lax.rsqrt(jnp.mean(x * x, axis=-1, keepdims=True) + EPS)
    kv = _dot((xn * gkv).astype(BF16), wkv)
    qm = _dot((xn * gmix).astype(BF16), winb)
    k = _group_rms(kv[:, :DIFF_W], kg, DIFF_HD)
    v = kv[:, DIFF_W:]
    q = _group_rms(qm[:, :DIFF_W], qg, DIFF_HD) * q_scale
    mq = _group_rms(qm[:, DIFF_W:], mqg, MEM_HD) * (MEM_HD ** -0.5)
    return k, v, q, mq


def _p_mixb_kernel(x_ref, gkv_ref, wkv_ref, kg_ref, gmix_ref, winb_ref, qg_ref, mqg_ref, mk_ref, mv_ref,
                   k_ref, v_ref, kb_ref, vb_ref, qb_ref, mb_ref):
    k, v, q, mq = _mixb_project(x_ref[0], gkv_ref[...], wkv_ref[...], kg_ref[...], gmix_ref[...],
                                winb_ref[...], qg_ref[...], mqg_ref[...], DIFF_HD ** -0.5 * LOG2_E)
    k_ref[0] = k
    v_ref[0] = v
    kb_ref[0] = k.astype(BF16)
    vb_ref[0] = v.astype(BF16)
    qb_ref[0] = q.astype(BF16)
    m = _mem_attend_shared(mq, mk_ref[0, 0].astype(BF16), mv_ref[0, 0].astype(BF16))
    mb_ref[0] = m.astype(BF16)


def _p_mixb(x, gkv, wkv, kg, gmix, winb, qg, mqg, mem_k, mem_v):
    batch, seq, _ = x.shape
    tile = ROW_TILE
    tok = lambda w: pl.BlockSpec((1, tile, w), lambda b, t: (b, t, 0))
    mem = pl.BlockSpec((1, 1, MEM_LEN, MEM_W), lambda b, t: (1, b, 0, 0))
    shp = lambda w, dt: jax.ShapeDtypeStruct((batch, seq, w), dt)
    return pl.pallas_call(
        _p_mixb_kernel,
        grid=(batch, seq // tile),
        in_specs=[tok(D_MODEL)] + [_const_spec(a.shape) for a in (gkv, wkv, kg, gmix, winb, qg, mqg)]
        + [mem, mem],
        out_specs=[tok(DIFF_W)] * 5 + [tok(MEM_W)],
        out_shape=[shp(DIFF_W, F32), shp(DIFF_W, F32), shp(DIFF_W, BF16), shp(DIFF_W, BF16),
                   shp(DIFF_W, BF16), shp(MEM_W, BF16)],
        compiler_params=_params(2),
        name="p_mixb",
    )(x, gkv, wkv, kg, gmix, winb, qg, mqg, mem_k, mem_v)


def _s_mixb_kernel(x_ref, gkv_ref, wkv_ref, kg_ref, gmix_ref, winb_ref, qg_ref, mqg_ref,
                   k_ref, v_ref, q_ref, mq_ref):
    k, v, q, mq = _mixb_project(x_ref[...], gkv_ref[...], wkv_ref[...], kg_ref[...], gmix_ref[...],
                                winb_ref[...], qg_ref[...], mqg_ref[...], DIFF_HD ** -0.5)
    k_ref[...] = k
    v_ref[...] = v
    q_ref[...] = q
    mq_ref[...] = mq


def _s_mixb(x, gkv, wkv, kg, gmix, winb, qg, mqg):
    n = x.shape[0]
    args = (x, gkv, wkv, kg, gmix, winb, qg, mqg)
    widths = (DIFF_W, DIFF_W, DIFF_W, MEM_W)
    return pl.pallas_call(
        _s_mixb_kernel,
        grid=(1,),
        in_specs=[_const_spec(a.shape) for a in args],
        out_specs=[_const_spec((n, w)) for w in widths],
        out_shape=[jax.ShapeDtypeStruct((n, w), F32) for w in widths],
        compiler_params=_params(1),
        name="s_mixb",
    )(*args)


def _p_att_kernel(q_ref, k_ref, v_ref, m_ref, x_ref, lam_ref, sg_ref, wout_ref, o_ref, c_ref):
    tq = q_ref.shape[1]
    qi = pl.program_id(1)
    lam = _lambda(lam_ref[...])
    row = lax.broadcasted_iota(jnp.int32, (tq, tq), 0)
    col = lax.broadcasted_iota(jnp.int32, (tq, tq), 1)
    causal = col <= row
    first_map = lax.broadcasted_iota(jnp.int32, (tq, DIFF_VD), 1) < DIFF_HD

    for h0 in range(0, DIFF_HEADS, ATT_HEADS_PER_TRIP):
        lanes = [slice(h * DIFF_VD, (h + 1) * DIFF_VD) for h in range(h0, h0 + ATT_HEADS_PER_TRIP)]
        qs = []
        for ln in lanes:
            q = q_ref[0, :, ln]
            zero = jnp.zeros_like(q)
            qs.append((jnp.where(first_map, q, zero), jnp.where(first_map, zero, q)))

        def block(j, carry, masked):
            start = pl.multiple_of(j * tq, tq)
            scores = []
            for i, ln in enumerate(lanes):
                kb = k_ref[0, pl.ds(start, tq), ln]
                scores += [_dot_nt(qs[i][c], kb) for c in range(2)]
            soft = []
            for n, s in enumerate(scores):
                m_old, l_old, _ = carry[n]
                if masked:
                    s = jnp.where(causal, s, NEG_INF)
                m_new = jnp.maximum(m_old, jnp.max(s, axis=-1, keepdims=True))
                alpha = jnp.exp2(m_old - m_new)
                p = jnp.exp2(s - m_new)
                soft.append((m_new, alpha * l_old + jnp.sum(p, axis=-1, keepdims=True), alpha,
                             p.astype(BF16)))
            new = []
            for n, (m_new, l_new, alpha, p) in enumerate(soft):
                vb = v_ref[0, pl.ds(start, tq), lanes[n // 2]]
                new.append((m_new, l_new, alpha * carry[n][2] + _dot(p, vb)))
            return tuple(new)

        init = tuple((jnp.full((tq, 1), NEG_INF, F32), jnp.zeros((tq, 1), F32),
                      jnp.zeros((tq, DIFF_VD), F32)) for _ in range(2 * ATT_HEADS_PER_TRIP))
        carry = lax.fori_loop(0, qi, lambda j, c: block(j, c, False), init)
        carry = block(qi, carry, True)
        for i, ln in enumerate(lanes):
            (_, l1, a1), (_, l2, a2) = carry[2 * i], carry[2 * i + 1]
            o = a1 / l1 - lam * (a2 / l2)
            o = _rms(o, sg_ref[:, ln]) * (1.0 - LAM_INIT_B)
            c_ref[:, ln] = o.astype(BF16)

    o_ref[0] = (x_ref[0] + _dot(c_ref[...], wout_ref[0:DIFF_W, :])
                + _dot(m_ref[0], wout_ref[DIFF_W:, :]))


def _p_att(q, k, v, m, x, lam, sg, wout):
    batch, seq, _ = x.shape
    tile = ROW_TILE
    tok = lambda w: pl.BlockSpec((1, tile, w), lambda b, t: (b, t, 0))
    kv = pl.BlockSpec((1, seq, DIFF_W), lambda b, t: (b, 0, 0), pipeline_mode=pl.Buffered(1))
    return pl.pallas_call(
        _p_att_kernel,
        grid=(batch, seq // tile),
        in_specs=[tok(DIFF_W), kv, kv, tok(MEM_W), tok(D_MODEL), _const_spec(lam.shape),
                  _const_spec(sg.shape), _const_spec(wout.shape)],
        out_specs=tok(D_MODEL),
        out_shape=jax.ShapeDtypeStruct(x.shape, F32),
        scratch_shapes=[pltpu.VMEM((tile, DIFF_W), BF16)],
        compiler_params=_params(2),
        name="p_att",
    )(q, k, v, m, x, lam, sg, wout)


def _s_mixa_kernel(x_ref, st_ref, gmix_ref, win_ref, cw_ref, cb_ref, lng_ref, lnb_ref, mqg_ref,
                   glu_ref, c_ref, mq_ref):
    n_t, nb, _ = x_ref.shape
    n_hist = st_ref.shape[0]
    x = x_ref[...].reshape(n_t * nb, D_MODEL)
    h = _rms(x, gmix_ref[...]).astype(BF16)
    z = _dot(h, win_ref[...])
    glu = z[:, :C_A] * jax.nn.sigmoid(z[:, C_A:2 * C_A])
    glu_ref[...] = glu.reshape(n_t, nb, C_A)
    mq = _group_rms(z[:, 2 * C_A:], mqg_ref[...], MEM_HD) * (MEM_HD ** -0.5)
    mq_ref[...] = mq.reshape(n_t, nb, MEM_W)

    ys = [[] for _ in range(n_t)]
    for l0 in range(0, C_A, CONV_LANES):
        lanes = slice(l0, l0 + CONV_LANES)
        acc = [jnp.broadcast_to(cb_ref[:, lanes], (nb, CONV_LANES)) for _ in range(n_t)]
        for r in range(n_hist + n_t):
            e = st_ref[r, :, lanes] if r < n_hist else glu[(r - n_hist) * nb:(r - n_hist + 1) * nb, lanes]
            for t in range(n_t):
                if 0 <= r - t < CONV_A_WIDTH:
                    acc[t] = acc[t] + cw_ref[r - t:r - t + 1, lanes] * e
        for t in range(n_t):
            ys[t].append(acc[t])
    for t in range(n_t):
        y = jnp.concatenate(ys[t], axis=-1)
        c_ref[t] = _silu(_layernorm(y, lng_ref[...], lnb_ref[...])).astype(BF16)


def _s_mixa(x_tm, st_tm, gmix, win, cw, cb, lng, lnb, mqg):
    n_t, n_seq, _ = x_tm.shape
    nb = 32
    blk = lambda r, w: pl.BlockSpec((r, nb, w), lambda i: (0, i, 0))
    consts = (gmix, win, cw, cb, lng, lnb, mqg)
    return pl.pallas_call(
        _s_mixa_kernel,
        grid=(n_seq // nb,),
        in_specs=[blk(n_t, D_MODEL), blk(st_tm.shape[0], C_A)] + [_const_spec(a.shape) for a in consts],
        out_specs=[blk(n_t, C_A), blk(n_t, C_A), blk(n_t, MEM_W)],
        out_shape=[jax.ShapeDtypeStruct((n_t, n_seq, C_A), F32),
                   jax.ShapeDtypeStruct((n_t, n_seq, C_A), BF16),
                   jax.ShapeDtypeStruct((n_t, n_seq, MEM_W), F32)],
        compiler_params=_params(1),
        name="s_mixa",
    )(x_tm, st_tm, *consts)


def _s_mem_kernel(q_ref, mk_ref, mv_ref, o_ref):
    nb = q_ref.shape[0]
    rows = MEM_HEADS * PAD_T
    head_of_row = jnp.right_shift(lax.broadcasted_iota(jnp.int32, (rows, MEM_W), 0), 3)
    own = head_of_row == _lane_group((rows, MEM_W), 6)
    head_of_lane = _lane_group((PAD_T, MEM_W), 6)
    for b in range(nb):
        q = q_ref[b]
        qb = jnp.where(own, jnp.concatenate([q] * MEM_HEADS, axis=0), 0.0).astype(BF16)
        s = _dot(qb, mk_ref[0, b].astype(BF16))
        p = jnp.exp(s - jnp.max(s, axis=-1, keepdims=True))
        o = _dot_nt(p.astype(BF16), mv_ref[0, b].astype(BF16)) / jnp.sum(p, axis=-1, keepdims=True)
        out = jnp.zeros((PAD_T, MEM_W), F32)
        for h in range(MEM_HEADS):
            out = jnp.where(head_of_lane == h, o[h * PAD_T:(h + 1) * PAD_T], out)
        o_ref[b] = out


def _s_mem(q_pad, mem_k, mem_v, layer):
    n_seq = q_pad.shape[0]
    nb = 8
    tok = pl.BlockSpec((nb, PAD_T, MEM_W), lambda i: (i, 0, 0))
    mem = pl.BlockSpec((1, nb, MEM_LEN, MEM_W), lambda i: (layer, i, 0, 0))
    return pl.pallas_call(
        _s_mem_kernel,
        grid=(n_seq // nb,),
        in_specs=[tok, mem, mem],
        out_specs=tok,
        out_shape=jax.ShapeDtypeStruct(q_pad.shape, F32),
        compiler_params=_params(1),
        name="s_mem",
    )(q_pad, mem_k, mem_v)


def _page_copies(pt_ref, ck_hbm, cv_hbm, kbuf, vbuf, sem, seq, slot, n_pages):
    copies = []
    for p in range(n_pages):
        page = pt_ref[seq * n_pages + p]
        copies.append(pltpu.make_async_copy(ck_hbm.at[page], kbuf.at[slot, :, p], sem.at[0, slot]))
        copies.append(pltpu.make_async_copy(cv_hbm.at[page], vbuf.at[slot, :, p], sem.at[1, slot]))
    return copies


def _s_paged_kernel(pt_ref, q_ref, kn_ref, vn_ref, lam_ref, sg_ref, ck_hbm, cv_hbm, o_ref,
                    kbuf, vbuf, sem, *, n_pages, n_new):
    seq = pl.program_id(0)
    slot = seq % 2
    fetch = functools.partial(_page_copies, pt_ref, ck_hbm, cv_hbm, kbuf, vbuf, sem, n_pages=n_pages)

    @pl.when(seq == 0)
    def _():
        for c in fetch(seq=0, slot=0):
            c.start()

    @pl.when(seq + 1 < pl.num_programs(0))
    def _():
        for c in fetch(seq=seq + 1, slot=1 - slot):
            c.start()

    for c in fetch(seq=seq, slot=slot):
        c.wait()
    lam = _lambda(lam_ref[...])
    first_map = lax.broadcasted_iota(jnp.int32, (PAD_T, DIFF_VD), 1) < DIFF_HD
    t_q = jnp.bitwise_and(lax.broadcasted_iota(jnp.int32, (2 * PAD_T, 1), 0), PAD_T - 1)
    n_past = n_pages * PAGE_SIZE

    head_lanes = [slice(h * DIFF_VD, (h + 1) * DIFF_VD) for h in range(DIFF_HEADS)]
    qfs, s_pasts = [], []
    for h, lanes in enumerate(head_lanes):
        q = q_ref[0, :, lanes]
        qf = jnp.concatenate([jnp.where(first_map, q, 0.0), jnp.where(first_map, 0.0, q)], axis=0)
        k_past = kbuf[slot, h].reshape(n_past, DIFF_VD).astype(BF16)
        qfs.append(qf)
        s_pasts.append(_dot_nt(qf.astype(BF16), k_past))

    weights, new_terms = [], []
    for h, lanes in enumerate(head_lanes):
        s_past, qf = s_pasts[h], qfs[h]
        s_new = [jnp.where(t <= t_q,
                           jnp.sum(qf * kn_ref[0, t:t + 1, lanes], axis=-1, keepdims=True), NEG_INF)
                 for t in range(n_new)]
        m = jnp.max(s_past, axis=-1, keepdims=True)
        for s in s_new:
            m = jnp.maximum(m, s)
        p_past = jnp.exp(s_past - m)
        p_new = [jnp.exp(s - m) for s in s_new]
        l = jnp.sum(p_past, axis=-1, keepdims=True)
        for p in p_new:
            l = l + p
        w1 = 1.0 / l[:PAD_T]
        w2 = lam / l[PAD_T:]
        weights.append((p_past[:PAD_T] * w1 - p_past[PAD_T:] * w2).astype(BF16))
        new_terms.append([p_new[t][:PAD_T] * w1 - p_new[t][PAD_T:] * w2 for t in range(n_new)])

    outs = []
    for h, lanes in enumerate(head_lanes):
        acc = _dot(weights[h], vbuf[slot, h].reshape(n_past, DIFF_VD).astype(BF16))
        for t in range(n_new):
            acc = acc + new_terms[h][t] * vn_ref[0, t:t + 1, lanes]
        outs.append(_rms(acc, sg_ref[:, lanes]) * (1.0 - LAM_INIT_B))
    o_ref[0] = jnp.concatenate(outs, axis=-1)


def _s_paged(page_table, q_pad, k_new, v_new, lam, sg, cache_k, cache_v):
    n_seq, n_pages = page_table.shape
    n_new = k_new.shape[1]
    tok = lambda r: pl.BlockSpec((1, r, DIFF_W), lambda b, pt: (b, 0, 0))
    hbm = pl.BlockSpec(memory_space=pl.ANY)
    page_buf = pltpu.VMEM((2, DIFF_HEADS, n_pages, PAGE_SIZE, DIFF_VD), F32)
    grid_spec = pltpu.PrefetchScalarGridSpec(
        num_scalar_prefetch=1,
        grid=(n_seq,),
        in_specs=[tok(PAD_T), tok(n_new), tok(n_new), _const_spec(lam.shape), _const_spec(sg.shape),
                  hbm, hbm],
        out_specs=tok(PAD_T),
        scratch_shapes=[page_buf, page_buf, pltpu.SemaphoreType.DMA((2, 2))],
    )
    return pl.pallas_call(
        functools.partial(_s_paged_kernel, n_pages=n_pages, n_new=n_new),
        grid_spec=grid_spec,
        out_shape=jax.ShapeDtypeStruct(q_pad.shape, F32),
        compiler_params=_params(1),
        name="s_paged",
    )(page_table.reshape(-1), q_pad, k_new, v_new, lam, sg, cache_k, cache_v)


def _to_time_major(a):
    return jnp.swapaxes(a, 0, 1)


def _pad_tokens(a_tm):
    a = jnp.swapaxes(a_tm, 0, 1)
    return jnp.pad(a, ((0, 0), (0, PAD_T - a.shape[1]), (0, 0)))


def _from_padded(a, n_t, dtype):
    return jnp.swapaxes(a[:, :n_t], 0, 1).astype(dtype)


def kernel(x_prompt, x_sample, mem_prompt, cache_mem_k, cache_mem_v, state_conv_a, state_ffn_conv, cache_k, cache_v, page_table, norm_mix, norm_ffn, norm_mem, w_mem_kv, mem_q_norm, mem_k_norm, w_in_a, conv_a_w, conv_a_b, ln_a_g, ln_a_b, w_out_a, norm_kv, w_kv, k_norm, w_in_b, q_norm, lambdas, subln_g, w_out_b, w_up, ffn_conv_w, ffn_conv_b, w_down):
    batch, seq, _ = x_prompt.shape
    n_seq, n_t, _ = x_sample.shape
    depth = norm_mix.shape[0]
    assert depth == 2 and w_in_a.shape[0] == 1 and w_in_b.shape[0] == 1
    assert seq % ROW_TILE == 0 and n_t <= PAD_T and n_seq % 32 == 0

    row = lambda a: a.reshape(1, -1)
    gmix = [row(norm_mix[l]) for l in range(depth)]
    gffn = [row(norm_ffn[l]) for l in range(depth)]
    mqg = [row(jnp.tile(mem_q_norm[l], MEM_HEADS)) for l in range(depth)]
    mkg = jnp.tile(mem_k_norm, (1, MEM_HEADS)).reshape(depth, 1, MEM_W)
    kg = row(jnp.tile(k_norm.reshape(-1), DIFF_HEADS))
    qg = row(jnp.tile(q_norm[0].reshape(-1), DIFF_HEADS))
    sg = row(jnp.tile(subln_g[0], DIFF_HEADS))
    lam = lambdas[0]
    win_a = w_in_a[0].astype(BF16)
    wout_a = w_out_a[0].astype(BF16)
    wkv = w_kv.astype(BF16)
    win_b = w_in_b[0].astype(BF16)
    wout_b = w_out_b[0].astype(BF16)
    wup = [w_up[l].astype(BF16) for l in range(depth)]
    wdn = [w_down[l].astype(BF16) for l in range(depth)]
    cw_a, cb_a = conv_a_w[0], row(conv_a_b[0])
    lng, lnb = row(ln_a_g[0]), row(ln_a_b[0])
    fcw = [ffn_conv_w[l] for l in range(depth)]
    fcb = [row(ffn_conv_b[l]) for l in range(depth)]

    p_mem_k, p_mem_v = _mem_kv(mem_prompt, norm_mem.reshape(depth, 1, D_MODEL), w_mem_kv.astype(BF16), mkg)
    x1, conv_tail = _p_mixa(x_prompt, gmix[0], win_a, cw_a, cb_a, lng, lnb, mqg[0], p_mem_k, p_mem_v, wout_a)
    x2, p_ffn0 = _p_ffn(x1, gffn[0], wup[0], fcw[0], fcb[0], wdn[0])
    p_k, p_v, kb, vb, qb, mb = _p_mixb(x2, row(norm_kv), wkv, kg, gmix[1], win_b, qg, mqg[1], p_mem_k, p_mem_v)
    x3 = _p_att(qb, kb, vb, mb, x2, lam, sg, wout_b)
    y_prompt, p_ffn1 = _p_ffn(x3, gffn[1], wup[1], fcw[1], fcb[1], wdn[1])

    hd = (MEM_HEADS, MEM_HD)
    p_mem_k = p_mem_k.reshape(depth, batch, MEM_LEN, *hd)
    p_mem_v = p_mem_v.reshape(depth, batch, MEM_LEN, *hd)
    p_conv_a = conv_tail[None, :, CARRY_ROWS - (CONV_A_WIDTH - 1):, :]
    p_ffn_conv = jnp.stack([p_ffn0, p_ffn1])
    p_k = p_k.reshape(batch, seq, DIFF_HEADS, DIFF_VD)
    p_v = p_v.reshape(batch, seq, DIFF_HEADS, DIFF_VD)

    smem_k = jnp.transpose(cache_mem_k, (0, 1, 3, 4, 2)).reshape(depth, n_seq, MEM_W, MEM_LEN)
    smem_v = jnp.transpose(cache_mem_v, (0, 1, 3, 4, 2)).reshape(depth, n_seq, MEM_W, MEM_LEN)
    xs = _to_time_major(x_sample)
    glu, c_a, mq_a = _s_mixa(xs, _to_time_major(state_conv_a[0]), gmix[0], win_a, cw_a, cb_a, lng, lnb, mqg[0])
    m_a = _from_padded(_s_mem(_pad_tokens(mq_a), smem_k, smem_v, 0), n_t, BF16)
    flat = lambda a: a.reshape(n_t * n_seq, a.shape[-1])
    xs2, s_ffn0 = _s_ffn(flat(xs), flat(c_a), flat(m_a), wout_a, gffn[0], wup[0], fcw[0], fcb[0], wdn[0],
                         _to_time_major(state_ffn_conv[0]))
    k_new, v_new, q_s, mq_b = _s_mixb(xs2, row(norm_kv), wkv, kg, gmix[1], win_b, qg, mqg[1])
    unflat = lambda a: a.reshape(n_t, n_seq, a.shape[-1])
    m_b = _from_padded(_s_mem(_pad_tokens(unflat(mq_b)), smem_k, smem_v, 1), n_t, BF16)
    k_new_sm = jnp.swapaxes(unflat(k_new), 0, 1)
    v_new_sm = jnp.swapaxes(unflat(v_new), 0, 1)
    c_b = _s_paged(page_table, _pad_tokens(unflat(q_s)), k_new_sm, v_new_sm, lam, sg,
                   jnp.swapaxes(cache_k, 1, 2), jnp.swapaxes(cache_v, 1, 2))
    c_b = _from_padded(c_b, n_t, BF16)
    ys, s_ffn1 = _s_ffn(xs2, flat(c_b), flat(m_b), wout_b, gffn[1], wup[1], fcw[1], fcb[1], wdn[1],
                        _to_time_major(state_ffn_conv[1]))

    y_sample = jnp.swapaxes(unflat(ys), 0, 1)
    s_conv_a = jnp.concatenate([state_conv_a[0][:, n_t:], jnp.swapaxes(glu, 0, 1)], axis=1)[None]
    s_ffn_conv = jnp.stack([_to_time_major(s_ffn0), _to_time_major(s_ffn1)])
    s_k = k_new_sm.reshape(n_seq, n_t, DIFF_HEADS, DIFF_VD)
    s_v = v_new_sm.reshape(n_seq, n_t, DIFF_HEADS, DIFF_VD)
    return (y_prompt, y_sample, p_mem_k, p_mem_v, p_conv_a, p_ffn_conv, p_k, p_v,
            s_conv_a, s_ffn_conv, s_k, s_v)
```
